```python
import jax, jax.numpy as jnp
from jax import lax
import numpy as np

D_MODEL = 2048
BATCH = 4
SEQ = 8192
DEPTH = 2

PLE_DIM = 256
N_HEADS = 16
Q_LORA = 512
KV_LORA = 512
QK_NOPE = 128
QK_ROPE = 64
QK_HEAD = QK_NOPE + QK_ROPE
V_HEAD = 128
ATT_WIDTH = N_HEADS * V_HEAD
ROPE_THETA = 10000.0
Q_BLOCK = 128
CONV_WIDTH = D_MODEL
CONV_K = 3
CONV_GROUPS = 16
N_MIXERS = 2
N_MLA = (DEPTH + 1) // 2
N_CONV = DEPTH // 2
EPS = 1e-6

kernel_name = "hybrid_mla_shortconv_ple"


def rmsnorm(x, g):
    x32 = x.astype(jnp.float32)
    y = x32 * lax.rsqrt(jnp.mean(x32 * x32, axis=-1, keepdims=True) + EPS)
    return (y * g.astype(jnp.float32)).astype(x.dtype)


def rope_tables(positions):
    inv_freq = ROPE_THETA ** (-jnp.arange(0, QK_ROPE, 2, dtype=jnp.float32) / QK_ROPE)
    ang = positions.astype(jnp.float32)[..., None] * inv_freq
    return jnp.cos(ang)[:, :, None, :], jnp.sin(ang)[:, :, None, :]


def apply_rope_tail(t, cos, sin):
    t_nope, t_rope = t[..., :QK_NOPE], t[..., QK_NOPE:]
    r1, r2 = jnp.split(t_rope, 2, axis=-1)
    c, s = cos.astype(t.dtype), sin.astype(t.dtype)
    rot = jnp.concatenate([r1 * c - r2 * s, r2 * c + r1 * s], axis=-1)
    return jnp.concatenate([t_nope, rot], axis=-1)


def causal_block_attention(q, k, v):
    bsz, s_len, h, dq = q.shape
    nb = s_len // Q_BLOCK
    qb = q.reshape(bsz, nb, Q_BLOCK, h, dq).transpose(1, 0, 2, 3, 4)
    k_pos = jnp.arange(s_len)
    scale = dq ** -0.5

    def one_block(args):
        q_blk, j = args
        s = jnp.einsum('bqhd,bkhd->bhqk', q_blk, k, preferred_element_type=jnp.float32) * scale
        q_pos = j * Q_BLOCK + jnp.arange(Q_BLOCK)
        mask = k_pos[None, :] <= q_pos[:, None]
        s = jnp.where(mask, s, -jnp.inf)
        pr = jax.nn.softmax(s, axis=-1).astype(v.dtype)
        return jnp.einsum('bhqk,bkhd->bqhd', pr, v)

    o = lax.map(one_block, (qb, jnp.arange(nb)))
    return o.transpose(1, 0, 2, 3, 4).reshape(bsz, s_len, h, v.shape[-1])


def mla_mixer(u, w_in, q_norm_g, kv_norm_g, w_q_up, w_kv_up, qn_g, kn_g, w_out, cos, sin):
    bsz, s_len, _ = u.shape
    z = u @ w_in
    c_q, c_kv, k_rope, gate = jnp.split(
        z, [Q_LORA, Q_LORA + KV_LORA, Q_LORA + KV_LORA + QK_ROPE], axis=-1)
    q = (rmsnorm(c_q, q_norm_g) @ w_q_up).reshape(bsz, s_len, N_HEADS, QK_HEAD)
    kv = (rmsnorm(c_kv, kv_norm_g) @ w_kv_up).reshape(bsz, s_len, N_HEADS, QK_NOPE + V_HEAD)
    k_nope, v = kv[..., :QK_NOPE], kv[..., QK_NOPE:]
    k_rope_h = jnp.broadcast_to(k_rope[:, :, None, :], (bsz, s_len, N_HEADS, QK_ROPE))
    k = jnp.concatenate([k_nope, k_rope_h], axis=-1)
    q = apply_rope_tail(rmsnorm(q, qn_g), cos, sin)
    k = apply_rope_tail(rmsnorm(k, kn_g), cos, sin)
    o = causal_block_attention(q, k, v).reshape(bsz, s_len, ATT_WIDTH)
    return (jax.nn.silu(gate) * o) @ w_out


def shortconv_mixer(u, w_in, conv_w, w_out):
    z = u @ w_in
    b_gate, c_gate, hid, gate = jnp.split(z, 4, axis=-1)
    t = c_gate * hid
    t = lax.conv_general_dilated(
        t, conv_w[:, None, :].astype(t.dtype), window_strides=(1,), padding=[(CONV_K - 1, 0)],
        dimension_numbers=('NWC', 'WIO', 'NWC'), feature_group_count=CONV_WIDTH)
    return (jax.nn.silu(gate) * b_gate * t) @ w_out


def setup_inputs(seed: int = 0) -> dict:
    key = jax.random.key(seed)
    ks = jax.random.split(key, 24)
    nrm = lambda k, shape, fan: jax.random.normal(k, shape, jnp.float32) * (fan ** -0.5)
    gain = lambda k, shape: 1.0 + 0.05 * jax.random.normal(k, shape, jnp.float32)
    x = jax.random.normal(ks[0], (BATCH, SEQ, D_MODEL), jnp.float32)
    p = jax.random.normal(ks[1], (DEPTH, BATCH, SEQ, PLE_DIM), jnp.float32)
    offset = jax.random.randint(ks[2], (BATCH, 1), 0, 1024, dtype=jnp.int32)
    positions = offset + jnp.arange(SEQ, dtype=jnp.int32)[None, :]
    norm_g = gain(ks[3], (DEPTH, D_MODEL))
    a_w_in = nrm(ks[4], (N_MLA, D_MODEL, Q_LORA + KV_LORA + QK_ROPE + ATT_WIDTH), D_MODEL)
    a_q_norm_g = gain(ks[5], (N_MLA, Q_LORA))
    a_kv_norm_g = gain(ks[6], (N_MLA, KV_LORA))
    a_w_q_up = nrm(ks[7], (N_MLA, Q_LORA, N_HEADS * QK_HEAD), Q_LORA)
    a_w_kv_up = nrm(ks[8], (N_MLA, KV_LORA, N_HEADS * (QK_NOPE + V_HEAD)), KV_LORA)
    a_qn_g = gain(ks[9], (N_MLA, QK_HEAD))
    a_kn_g = gain(ks[10], (N_MLA, QK_HEAD))
    a_w_out = nrm(ks[11], (N_MLA, ATT_WIDTH, D_MODEL), ATT_WIDTH)
    b_w_in = nrm(ks[12], (N_CONV, D_MODEL, 4 * CONV_WIDTH), D_MODEL)
    b_conv_w = nrm(ks[13], (N_CONV, CONV_K, CONV_WIDTH), CONV_K)
    b_w_out = nrm(ks[14], (N_CONV, CONV_WIDTH, D_MODEL), CONV_WIDTH)
    ple_w_proj = nrm(ks[15], (DEPTH, PLE_DIM, D_MODEL), PLE_DIM)
    ple_norm_g = gain(ks[16], (DEPTH, D_MODEL))
    ple_w_gate = nrm(ks[17], (DEPTH, D_MODEL, D_MODEL), D_MODEL)
    ple_b_gate = 0.02 * jax.random.normal(ks[18], (DEPTH, D_MODEL), jnp.float32)
    return {"x": x, "p": p, "positions": positions, "norm_g": norm_g,
            "a_w_in": a_w_in, "a_q_norm_g": a_q_norm_g, "a_kv_norm_g": a_kv_norm_g,
            "a_w_q_up": a_w_q_up, "a_w_kv_up": a_w_kv_up, "a_qn_g": a_qn_g, "a_kn_g": a_kn_g,
            "a_w_out": a_w_out, "b_w_in": b_w_in, "b_conv_w": b_conv_w, "b_w_out": b_w_out,
            "ple_w_proj": ple_w_proj, "ple_norm_g": ple_norm_g, "ple_w_gate": ple_w_gate,
            "ple_b_gate": ple_b_gate}


def reference(x, p, positions, norm_g, a_w_in, a_q_norm_g, a_kv_norm_g, a_w_q_up, a_w_kv_up,
              a_qn_g, a_kn_g, a_w_out, b_w_in, b_conv_w, b_w_out, ple_w_proj, ple_norm_g,
              ple_w_gate, ple_b_gate):
    cos, sin = rope_tables(positions)
    h = x
    for i in range(DEPTH):
        u = rmsnorm(h, norm_g[i])
        j = i // N_MIXERS
        if i % N_MIXERS == 0:
            y = mla_mixer(u, a_w_in[j], a_q_norm_g[j], a_kv_norm_g[j], a_w_q_up[j],
                          a_w_kv_up[j], a_qn_g[j], a_kn_g[j], a_w_out[j], cos, sin)
        else:
            y = shortconv_mixer(u, b_w_in[j], b_conv_w[j], b_w_out[j])
        h = h + y
        e = rmsnorm(p[i] @ ple_w_proj[i], ple_norm_g[i])
        h = h + jax.nn.sigmoid(h @ ple_w_gate[i] + ple_b_gate[i]) * e
    return h
```

```python
import functools

import jax
import jax.numpy as jnp
from jax import lax
from jax.experimental import pallas as pl
from jax.experimental.pallas import tpu as pltpu

EPS = 1e-6
ROPE_THETA = 10000.0
MASK_VALUE = -1e30
VMEM_LIMIT_BYTES = 56 * 1024 * 1024

_NT = (((1,), (1,)), ((), ()))


def _tile_plan(s_len, conv_width):
    return min(512, s_len), min(512, s_len), min(1024, s_len), min(512, conv_width)


def _params(semantics):
    return pltpu.CompilerParams(dimension_semantics=semantics,
                                vmem_limit_bytes=VMEM_LIMIT_BYTES)


def _resident(shape):
    nd = len(shape)
    return pl.BlockSpec(shape, lambda *_: (0,) * nd, pipeline_mode=pl.Buffered(1))


def _rms_scale(x):
    return lax.rsqrt(jnp.mean(x * x, axis=-1, keepdims=True) + EPS)


def _sigmoid(x):
    return 1.0 / (1.0 + jnp.exp(-x))


def _rope_kernel(pos_row_ref, pos_col_ref, invf_col_ref, invf_row_ref, sign_row_ref,
                 cos_t_ref, sin_t_ref, cos_ref, sin_ref):
    ang_t = invf_col_ref[...] * pos_row_ref[0].astype(jnp.float32)
    cos_t_ref[0] = jnp.cos(ang_t)
    sin_t_ref[0] = jnp.sin(ang_t)
    ang = pos_col_ref[0].astype(jnp.float32) * invf_row_ref[...]
    cos_ref[0] = jnp.cos(ang)
    sin_ref[0] = jnp.sin(ang) * sign_row_ref[...]


def _rope_tables(positions, rope_dim):
    bsz, s_len = positions.shape
    half = rope_dim // 2
    ts = min(s_len, 1024)
    inv_freq = ROPE_THETA ** (-jnp.arange(0, rope_dim, 2, dtype=jnp.float32) / rope_dim)
    invf_col = inv_freq[:, None]
    invf_row = jnp.concatenate([inv_freq, inv_freq])[None, :]
    sign_row = jnp.concatenate([-jnp.ones((half,), jnp.float32),
                                jnp.ones((half,), jnp.float32)])[None, :]
    return pl.pallas_call(
        _rope_kernel,
        grid=(bsz, s_len // ts),
        in_specs=[
            pl.BlockSpec((1, 1, ts), lambda b, i: (b, 0, i)),
            pl.BlockSpec((1, ts, 1), lambda b, i: (b, i, 0)),
            pl.BlockSpec((half, 1), lambda b, i: (0, 0)),
            pl.BlockSpec((1, rope_dim), lambda b, i: (0, 0)),
            pl.BlockSpec((1, rope_dim), lambda b, i: (0, 0)),
        ],
        out_specs=[
            pl.BlockSpec((1, half, ts), lambda b, i: (b, 0, i)),
            pl.BlockSpec((1, half, ts), lambda b, i: (b, 0, i)),
            pl.BlockSpec((1, ts, rope_dim), lambda b, i: (b, i, 0)),
            pl.BlockSpec((1, ts, rope_dim), lambda b, i: (b, i, 0)),
        ],
        out_shape=[
            jax.ShapeDtypeStruct((bsz, half, s_len), jnp.float32),
            jax.ShapeDtypeStruct((bsz, half, s_len), jnp.float32),
            jax.ShapeDtypeStruct((bsz, s_len, rope_dim), jnp.float32),
            jax.ShapeDtypeStruct((bsz, s_len, rope_dim), jnp.float32),
        ],
        compiler_params=_params(("parallel", "parallel")),
        name="rope_tables",
    )(positions[:, None, :], positions[:, :, None], invf_col, invf_row, sign_row)


def _mla_in_kernel(h_ref, g_ref, w_lat_ref, w_gate_ref, lat_ref, sgate_ref):
    h = h_ref[...]
    u = (h * _rms_scale(h) * g_ref[...]).astype(jnp.bfloat16)
    lat_ref[...] = jnp.dot(u, w_lat_ref[...], preferred_element_type=jnp.float32)
    gate = jnp.dot(u, w_gate_ref[...], preferred_element_type=jnp.float32)
    sgate_ref[...] = (gate * _sigmoid(gate)).astype(jnp.bfloat16)


def _mla_in(h2d, g, w_lat, w_gate, tm):
    n_tok, d = h2d.shape
    n_lat, n_gate = w_lat.shape[1], w_gate.shape[1]
    return pl.pallas_call(
        _mla_in_kernel,
        grid=(n_tok // tm,),
        in_specs=[
            pl.BlockSpec((tm, d), lambda i: (i, 0)),
            _resident((1, d)),
            _resident((d, n_lat)),
            _resident((d, n_gate)),
        ],
        out_specs=[
            pl.BlockSpec((tm, n_lat), lambda i: (i, 0)),
            pl.BlockSpec((tm, n_gate), lambda i: (i, 0)),
        ],
        out_shape=[
            jax.ShapeDtypeStruct((n_tok, n_lat), jnp.float32),
            jax.ShapeDtypeStruct((n_tok, n_gate), jnp.bfloat16),
        ],
        compiler_params=_params(("parallel",)),
        name="mla_in_proj",
    )(h2d, g, w_lat, w_gate)


def _qkv_kernel(dims, lat_ref, cos_t_ref, sin_t_ref, cos_ref, sin_ref,
                gq_lat_ref, gkv_lat_ref, wq_t_ref, wk_ref, wv_t_ref,
                gq_ref, gk_nope_ref, gk_rope_ref, gk_rope_sw_ref,
                q_out, k_out, v_out, q_scr, k_scr):
    n_heads, q_lora, kv_lora, nope, rope, v_dim = dims
    qk = nope + rope
    half = rope // 2
    lat = lat_ref[...]
    c_q = lat[:, :q_lora]
    c_kv = lat[:, q_lora:q_lora + kv_lora]
    k_r = lat[:, q_lora + kv_lora:q_lora + kv_lora + rope]
    k_r_sw = lat[:, q_lora + kv_lora + rope:q_lora + kv_lora + 2 * rope]

    cq_n = (c_q * _rms_scale(c_q) * gq_lat_ref[...]).astype(jnp.bfloat16)
    ckv_n = (c_kv * _rms_scale(c_kv) * gkv_lat_ref[...]).astype(jnp.bfloat16)

    q_scr[...] = lax.dot_general(wq_t_ref[...], cq_n, _NT, preferred_element_type=jnp.float32)
    cos_t = cos_t_ref[0]
    sin_t = sin_t_ref[0]
    gq = gq_ref[...]
    for hd in range(n_heads):
        blk = q_scr[hd * qk:(hd + 1) * qk, :]
        rinv = lax.rsqrt(jnp.mean(blk * blk, axis=0, keepdims=True) + EPS)
        qn = blk * rinv * gq
        r1 = qn[nope:nope + half]
        r2 = qn[nope + half:]
        q_out[0, hd, :nope, :] = qn[:nope].astype(jnp.bfloat16)
        q_out[0, hd, nope:nope + half, :] = (r1 * cos_t - r2 * sin_t).astype(jnp.bfloat16)
        q_out[0, hd, nope + half:, :] = (r2 * cos_t + r1 * sin_t).astype(jnp.bfloat16)

    v_t = lax.dot_general(wv_t_ref[...], ckv_n, _NT, preferred_element_type=jnp.float32)
    v_out[0] = v_t.astype(jnp.bfloat16).reshape(n_heads, v_dim, v_t.shape[-1])

    k_scr[...] = jnp.dot(ckv_n, wk_ref[...], preferred_element_type=jnp.float32)
    ss_rope = jnp.sum(k_r * k_r, axis=-1, keepdims=True)
    rot = (k_r * gk_rope_ref[...]) * cos_ref[0] + (k_r_sw * gk_rope_sw_ref[...]) * sin_ref[0]
    gk_nope = gk_nope_ref[...]
    for hd in range(n_heads):
        blk = k_scr[:, hd * nope:(hd + 1) * nope]
        ss = jnp.sum(blk * blk, axis=-1, keepdims=True) + ss_rope
        rinv = lax.rsqrt(ss * (1.0 / qk) + EPS)
        k_out[0, hd, :, :nope] = (blk * rinv * gk_nope).astype(jnp.bfloat16)
        k_out[0, hd, :, nope:] = (rot * rinv).astype(jnp.bfloat16)


def _qkv(lat, tables, gq_lat, gkv_lat, wq_t, wk, wv_t, gq_b, gk_nope, gk_rope, gk_rope_sw,
         dims, bsz, s_len, tm):
    n_heads, q_lora, kv_lora, nope, rope, v_dim = dims
    qk = nope + rope
    half = rope // 2
    cos_t, sin_t, cos, sin = tables
    spb = s_len // tm
    n_lat = lat.shape[1]
    tok = lambda i: (i // spb, i % spb)
    return pl.pallas_call(
        functools.partial(_qkv_kernel, dims),
        grid=(bsz * spb,),
        in_specs=[
            pl.BlockSpec((tm, n_lat), lambda i: (i, 0)),
            pl.BlockSpec((1, half, tm), lambda i: (i // spb, 0, i % spb)),
            pl.BlockSpec((1, half, tm), lambda i: (i // spb, 0, i % spb)),
            pl.BlockSpec((1, tm, rope), lambda i: (i // spb, i % spb, 0)),
            pl.BlockSpec((1, tm, rope), lambda i: (i // spb, i % spb, 0)),
            _resident(gq_lat.shape),
            _resident(gkv_lat.shape),
            _resident(wq_t.shape),
            _resident(wk.shape),
            _resident(wv_t.shape),
            _resident(gq_b.shape),
            _resident(gk_nope.shape),
            _resident(gk_rope.shape),
            _resident(gk_rope_sw.shape),
        ],
        out_specs=[
            pl.BlockSpec((1, n_heads, qk, tm), lambda i: (i // spb, 0, 0, i % spb)),
            pl.BlockSpec((1, n_heads, tm, qk), lambda i: (i // spb, 0, i % spb, 0)),
            pl.BlockSpec((1, n_heads, v_dim, tm), lambda i: (i // spb, 0, 0, i % spb)),
        ],
        out_shape=[
            jax.ShapeDtypeStruct((bsz, n_heads, qk, s_len), jnp.bfloat16),
            jax.ShapeDtypeStruct((bsz, n_heads, s_len, qk), jnp.bfloat16),
            jax.ShapeDtypeStruct((bsz, n_heads, v_dim, s_len), jnp.bfloat16),
        ],
        scratch_shapes=[
            pltpu.VMEM((n_heads * qk, tm), jnp.float32),
            pltpu.VMEM((tm, n_heads * nope), jnp.float32),
        ],
        compiler_params=_params(("parallel",)),
        name="mla_qkv",
    )(lat, cos_t, sin_t, cos, sin, gq_lat, gkv_lat, wq_t, wk, wv_t,
      gq_b, gk_nope, gk_rope, gk_rope_sw)


def _attn_kernel(blk, q_ref, k_ref, v_ref, sg_ref, o_ref, m_ref, l_ref, acc_ref):
    qi = pl.program_id(2)
    q_t = q_ref[0, 0]
    m_ref[...] = jnp.full(m_ref.shape, MASK_VALUE, jnp.float32)
    l_ref[...] = jnp.zeros(l_ref.shape, jnp.float32)
    acc_ref[...] = jnp.zeros(acc_ref.shape, jnp.float32)

    def step(j, masked):
        start = pl.multiple_of(j * blk, blk)
        k_blk = k_ref[0, 0, pl.ds(start, blk), :]
        s_t = jnp.dot(k_blk, q_t, preferred_element_type=jnp.float32)
        if masked:
            kv_pos = lax.broadcasted_iota(jnp.int32, s_t.shape, 0)
            q_pos = lax.broadcasted_iota(jnp.int32, s_t.shape, 1)
            s_t = jnp.where(kv_pos <= q_pos, s_t, MASK_VALUE)
        m_prev = m_ref[...]
        m_new = jnp.maximum(m_prev, jnp.max(s_t, axis=0, keepdims=True))
        p = jnp.exp(s_t - m_new)
        alpha = jnp.exp(m_prev - m_new)
        l_ref[...] = alpha * l_ref[...] + jnp.sum(p, axis=0, keepdims=True)
        v_blk = v_ref[0, 0, :, pl.ds(start, blk)]
        pv = jnp.dot(v_blk, p.astype(jnp.bfloat16), preferred_element_type=jnp.float32)
        acc_ref[...] = alpha * acc_ref[...] + pv
        m_ref[...] = m_new

    def body(j, carry):
        step(j, False)
        return carry

    lax.fori_loop(0, qi, body, 0)
    step(qi, True)

    o_t = acc_ref[...] * (1.0 / l_ref[...])
    o = o_t.T
    o_ref[0] = (o * sg_ref[0].astype(jnp.float32)).astype(jnp.bfloat16)


def _attention(q_t, k, v_t, sgate, blk):
    bsz, n_heads, qk, s_len = q_t.shape
    v_dim = v_t.shape[2]
    return pl.pallas_call(
        functools.partial(_attn_kernel, blk),
        grid=(bsz, n_heads, s_len // blk),
        in_specs=[
            pl.BlockSpec((1, 1, qk, blk), lambda b, h, i: (b, h, 0, i)),
            pl.BlockSpec((1, 1, s_len, qk), lambda b, h, i: (b, h, 0, 0)),
            pl.BlockSpec((1, 1, v_dim, s_len), lambda b, h, i: (b, h, 0, 0)),
            pl.BlockSpec((1, blk, v_dim), lambda b, h, i: (b, i, h)),
        ],
        out_specs=pl.BlockSpec((1, blk, v_dim), lambda b, h, i: (b, i, h)),
        out_shape=jax.ShapeDtypeStruct((bsz, s_len, n_heads * v_dim), jnp.bfloat16),
        scratch_shapes=[
            pltpu.VMEM((1, blk), jnp.float32),
            pltpu.VMEM((1, blk), jnp.float32),
            pltpu.VMEM((v_dim, blk), jnp.float32),
        ],
        compiler_params=_params(("parallel", "parallel", "arbitrary")),
        name="mla_attention",
    )(q_t, k, v_t, sgate)


def _out_ple_kernel(act_ref, h_ref, p_ref, w_out_ref, w_proj_ref, g_ref, w_gate_ref, b_ref, o_ref):
    y = jnp.dot(act_ref[...], w_out_ref[...], preferred_element_type=jnp.float32)
    h1 = h_ref[...] + y
    e = jnp.dot(p_ref[...].astype(jnp.bfloat16), w_proj_ref[...], preferred_element_type=jnp.float32)
    e = e * _rms_scale(e) * g_ref[...]
    gate = jnp.dot(h1.astype(jnp.bfloat16), w_gate_ref[...], preferred_element_type=jnp.float32)
    o_ref[...] = h1 + _sigmoid(gate + b_ref[...]) * e


def _out_ple(act, h2d, p2d, w_out, w_proj, g, w_gate, b, tm):
    n_tok, d = h2d.shape
    width = act.shape[1]
    ple = p2d.shape[1]
    return pl.pallas_call(
        _out_ple_kernel,
        grid=(n_tok // tm,),
        in_specs=[
            pl.BlockSpec((tm, width), lambda i: (i, 0)),
            pl.BlockSpec((tm, d), lambda i: (i, 0)),
            pl.BlockSpec((tm, ple), lambda i: (i, 0)),
            _resident(w_out.shape),
            _resident(w_proj.shape),
            _resident(g.shape),
            _resident(w_gate.shape),
            _resident(b.shape),
        ],
        out_specs=pl.BlockSpec((tm, d), lambda i: (i, 0)),
        out_shape=jax.ShapeDtypeStruct((n_tok, d), jnp.float32),
        compiler_params=_params(("parallel",)),
        name="out_proj_ple",
    )(act, h2d, p2d, w_out, w_proj, g, w_gate, b)


def _conv_in_kernel(tiles_per_seq, h_ref, g_ref, wb_ref, wc_ref, wh_ref, wg_ref, cw_ref,
                    o_ref, u_scr, carry_scr):
    i = pl.program_id(0)
    n = pl.program_id(1)

    @pl.when(n == 0)
    def _():
        h = h_ref[...]
        u_scr[...] = (h * _rms_scale(h) * g_ref[...]).astype(jnp.bfloat16)

    @pl.when(i % tiles_per_seq == 0)
    def _():
        carry_scr[n] = jnp.zeros(carry_scr.shape[1:], jnp.float32)

    u = u_scr[...]
    b_gate = jnp.dot(u, wb_ref[...], preferred_element_type=jnp.float32)
    c_gate = jnp.dot(u, wc_ref[...], preferred_element_type=jnp.float32)
    hid = jnp.dot(u, wh_ref[...], preferred_element_type=jnp.float32)
    gate = jnp.dot(u, wg_ref[...], preferred_element_type=jnp.float32)
    t = c_gate * hid
    tm = t.shape[0]
    carry = carry_scr[n]
    prev1 = carry[7:8, :]
    prev2 = carry[6:7, :]
    row = lax.broadcasted_iota(jnp.int32, t.shape, 0)
    t1 = jnp.where(row == 0, prev1, pltpu.roll(t, 1, 0))
    t2 = jnp.where(row == 0, prev2, jnp.where(row == 1, prev1, pltpu.roll(t, 2, 0)))
    carry_scr[n] = t[tm - 8:, :]
    cw = cw_ref[...]
    conv = cw[2:3, :] * t + cw[1:2, :] * t1 + cw[0:1, :] * t2
    o_ref[...] = (gate * _sigmoid(gate) * b_gate * conv).astype(jnp.bfloat16)


def _conv_in(h2d, g, w_in, conv_w, s_len, tm, tn):
    n_tok, d = h2d.shape
    width = w_in.shape[1] // 4
    nb = width // tn
    w_spec = lambda part: pl.BlockSpec((d, tn), lambda i, n: (0, part * nb + n))
    return pl.pallas_call(
        functools.partial(_conv_in_kernel, s_len // tm),
        grid=(n_tok // tm, nb),
        in_specs=[
            pl.BlockSpec((tm, d), lambda i, n: (i, 0)),
            pl.BlockSpec((1, d), lambda i, n: (0, 0)),
            w_spec(0), w_spec(1), w_spec(2), w_spec(3),
            pl.BlockSpec((conv_w.shape[0], tn), lambda i, n: (0, n)),
        ],
        out_specs=pl.BlockSpec((tm, tn), lambda i, n: (i, n)),
        out_shape=jax.ShapeDtypeStruct((n_tok, width), jnp.bfloat16),
        scratch_shapes=[
            pltpu.VMEM((tm, d), jnp.bfloat16),
            pltpu.VMEM((nb, 8, tn), jnp.float32),
        ],
        compiler_params=_params(("arbitrary", "arbitrary")),
        name="conv_in_proj",
    )(h2d, g, w_in, w_in, w_in, w_in, conv_w)


def kernel(x, p, positions, norm_g, a_w_in, a_q_norm_g, a_kv_norm_g, a_w_q_up, a_w_kv_up,
           a_qn_g, a_kn_g, a_w_out, b_w_in, b_conv_w, b_w_out, ple_w_proj, ple_norm_g,
           ple_w_gate, ple_b_gate):
    bsz, s_len, d = x.shape
    depth = p.shape[0]
    ple_dim = p.shape[-1]
    assert depth == 2 and a_w_in.shape[0] == 1 and b_w_in.shape[0] == 1
    q_lora = a_q_norm_g.shape[-1]
    kv_lora = a_kv_norm_g.shape[-1]
    qk = a_qn_g.shape[-1]
    att_width = a_w_out.shape[1]
    rope = a_w_in.shape[-1] - q_lora - kv_lora - att_width
    nope = qk - rope
    n_heads = a_w_q_up.shape[-1] // qk
    v_dim = a_w_kv_up.shape[-1] // n_heads - nope
    assert n_heads * v_dim == att_width
    half = rope // 2
    n_tok = bsz * s_len
    bf = jnp.bfloat16

    tm, blk, tm_conv, tn_conv = _tile_plan(s_len, b_w_out.shape[1])
    assert s_len % tm == 0 and s_len % blk == 0 and s_len % tm_conv == 0

    w_in0 = a_w_in[0]
    n_lat = q_lora + kv_lora
    w_rope = w_in0[:, n_lat:n_lat + rope]
    w_rope_sw = jnp.concatenate([w_rope[:, half:], w_rope[:, :half]], axis=1)
    w_lat = jnp.concatenate([w_in0[:, :n_lat], w_rope, w_rope_sw], axis=1).astype(bf)
    w_gate0 = w_in0[:, n_lat + rope:].astype(bf)
    wq_t = a_w_q_up[0].T.astype(bf)
    w_kv = a_w_kv_up[0].reshape(kv_lora, n_heads, nope + v_dim)
    wk = w_kv[:, :, :nope].reshape(kv_lora, n_heads * nope).astype(bf)
    wv_t = w_kv[:, :, nope:].reshape(kv_lora, n_heads * v_dim).T.astype(bf)
    gq_b = jnp.broadcast_to((a_qn_g[0] * (qk ** -0.5))[:, None], (qk, tm))
    gk = a_kn_g[0]
    gk_nope = gk[None, :nope]
    gk_rope = gk[None, nope:]
    gk_rope_sw = jnp.concatenate([gk[nope + half:], gk[nope:nope + half]])[None, :]

    h = x.reshape(n_tok, d)
    p2 = p.reshape(depth, n_tok, ple_dim)

    tables = _rope_tables(positions, rope)
    lat, sgate = _mla_in(h, norm_g[0][None, :], w_lat, w_gate0, tm)
    dims = (n_heads, q_lora, kv_lora, nope, rope, v_dim)
    q_t, k, v_t = _qkv(lat, tables, a_q_norm_g[0][None, :], a_kv_norm_g[0][None, :],
                       wq_t, wk, wv_t, gq_b, gk_nope, gk_rope, gk_rope_sw,
                       dims, bsz, s_len, tm)
    act = _attention(q_t, k, v_t, sgate.reshape(bsz, s_len, att_width), blk)
    h = _out_ple(act.reshape(n_tok, att_width), h, p2[0], a_w_out[0].astype(bf),
                 ple_w_proj[0].astype(bf), ple_norm_g[0][None, :], ple_w_gate[0].astype(bf),
                 ple_b_gate[0][None, :], tm)

    act = _conv_in(h, norm_g[1][None, :], b_w_in[0].astype(bf), b_conv_w[0], s_len,
                   tm_conv, tn_conv)
    h = _out_ple(act, h, p2[1], b_w_out[0].astype(bf), ple_w_proj[1].astype(bf),
                 ple_norm_g[1][None, :], ple_w_gate[1].astype(bf), ple_b_gate[1][None, :], tm)
    return h.reshape(bsz, s_len, d)
```

```python
import functools

import jax
import jax.numpy as jnp
from jax import lax
from jax.experimental import pallas as pl
from jax.experimental.pallas import tpu as pltpu

EPS = 1e-6
ROPE_THETA = 10000.0
MASK_VALUE = -1e30
LOG2_E = 1.4426950408889634
ONES_ROWS = 16
VMEM_LIMIT_BYTES = 56 * 1024 * 1024

_NT = (((1,), (1,)), ((), ()))


def _tile_plan(s_len, conv_width):
    return min(512, s_len), min(512, s_len), min(512, s_len), min(1024, s_len), min(512, conv_width)


def _params(semantics):
    return pltpu.CompilerParams(dimension_semantics=semantics,
                                vmem_limit_bytes=VMEM_LIMIT_BYTES)


def _resident(shape):
    nd = len(shape)
    return pl.BlockSpec(shape, lambda *_: (0,) * nd, pipeline_mode=pl.Buffered(1))


def _rms_scale(x):
    return lax.rsqrt(jnp.mean(x * x, axis=-1, keepdims=True) + EPS)


def _sigmoid(x):
    return 1.0 / (1.0 + jnp.exp(-x))


def _rope_kernel(pos_row_ref, pos_col_ref, invf_col_ref, invf_row_ref, sign_row_ref,
                 cos_t_ref, sin_t_ref, cos_ref, sin_ref):
    ang_t = invf_col_ref[...] * pos_row_ref[0].astype(jnp.float32)
    cos_t_ref[0] = jnp.cos(ang_t)
    sin_t_ref[0] = jnp.sin(ang_t)
    ang = pos_col_ref[0].astype(jnp.float32) * invf_row_ref[...]
    cos_ref[0] = jnp.cos(ang)
    sin_ref[0] = jnp.sin(ang) * sign_row_ref[...]


def _rope_tables(positions, rope_dim):
    bsz, s_len = positions.shape
    half = rope_dim // 2
    ts = min(s_len, 1024)
    inv_freq = ROPE_THETA ** (-jnp.arange(0, rope_dim, 2, dtype=jnp.float32) / rope_dim)
    invf_col = inv_freq[:, None]
    invf_row = jnp.concatenate([inv_freq, inv_freq])[None, :]
    sign_row = jnp.concatenate([-jnp.ones((half,), jnp.float32),
                                jnp.ones((half,), jnp.float32)])[None, :]
    return pl.pallas_call(
        _rope_kernel,
        grid=(bsz, s_len // ts),
        in_specs=[
            pl.BlockSpec((1, 1, ts), lambda b, i: (b, 0, i)),
            pl.BlockSpec((1, ts, 1), lambda b, i: (b, i, 0)),
            pl.BlockSpec((half, 1), lambda b, i: (0, 0)),
            pl.BlockSpec((1, rope_dim), lambda b, i: (0, 0)),
            pl.BlockSpec((1, rope_dim), lambda b, i: (0, 0)),
        ],
        out_specs=[
            pl.BlockSpec((1, half, ts), lambda b, i: (b, 0, i)),
            pl.BlockSpec((1, half, ts), lambda b, i: (b, 0, i)),
            pl.BlockSpec((1, ts, rope_dim), lambda b, i: (b, i, 0)),
            pl.BlockSpec((1, ts, rope_dim), lambda b, i: (b, i, 0)),
        ],
        out_shape=[
            jax.ShapeDtypeStruct((bsz, half, s_len), jnp.float32),
            jax.ShapeDtypeStruct((bsz, half, s_len), jnp.float32),
            jax.ShapeDtypeStruct((bsz, s_len, rope_dim), jnp.float32),
            jax.ShapeDtypeStruct((bsz, s_len, rope_dim), jnp.float32),
        ],
        compiler_params=_params(("parallel", "parallel")),
        name="rope_tables",
    )(positions[:, None, :], positions[:, :, None], invf_col, invf_row, sign_row)


def _mla_in_kernel(h_ref, g_ref, w_lat_ref, w_gate_ref, lat_ref, sgate_ref):
    h = h_ref[...]
    u = (h * _rms_scale(h) * g_ref[...]).astype(jnp.bfloat16)
    lat_ref[...] = jnp.dot(u, w_lat_ref[...], preferred_element_type=jnp.float32)
    gate = jnp.dot(u, w_gate_ref[...], preferred_element_type=jnp.float32)
    sgate_ref[...] = (gate * _sigmoid(gate)).astype(jnp.bfloat16)


def _mla_in(h2d, g, w_lat, w_gate, tm):
    n_tok, d = h2d.shape
    n_lat, n_gate = w_lat.shape[1], w_gate.shape[1]
    return pl.pallas_call(
        _mla_in_kernel,
        grid=(n_tok // tm,),
        in_specs=[
            pl.BlockSpec((tm, d), lambda i: (i, 0)),
            _resident((1, d)),
            _resident((d, n_lat)),
            _resident((d, n_gate)),
        ],
        out_specs=[
            pl.BlockSpec((tm, n_lat), lambda i: (i, 0)),
            pl.BlockSpec((tm, n_gate), lambda i: (i, 0)),
        ],
        out_shape=[
            jax.ShapeDtypeStruct((n_tok, n_lat), jnp.float32),
            jax.ShapeDtypeStruct((n_tok, n_gate), jnp.bfloat16),
        ],
        compiler_params=_params(("parallel",)),
        name="mla_in_proj",
    )(h2d, g, w_lat, w_gate)


def _qkv_kernel(dims, lat_ref, cos_t_ref, sin_t_ref, cos_ref, sin_ref,
                gq_lat_ref, gkv_lat_ref, wq_t_ref, wk_ref, wv_t_ref,
                gq_ref, gk_nope_ref, gk_rope_ref, gk_rope_sw_ref,
                q_out, k_out, v_out, q_scr, k_scr):
    n_heads, q_lora, kv_lora, nope, rope, v_dim = dims
    qk = nope + rope
    half = rope // 2
    lat = lat_ref[...]
    c_q = lat[:, :q_lora]
    c_kv = lat[:, q_lora:q_lora + kv_lora]
    k_r = lat[:, q_lora + kv_lora:q_lora + kv_lora + rope]
    k_r_sw = lat[:, q_lora + kv_lora + rope:q_lora + kv_lora + 2 * rope]

    cq_n = (c_q * _rms_scale(c_q) * gq_lat_ref[...]).astype(jnp.bfloat16)
    ckv_n = (c_kv * _rms_scale(c_kv) * gkv_lat_ref[...]).astype(jnp.bfloat16)

    q_scr[...] = lax.dot_general(wq_t_ref[...], cq_n, _NT, preferred_element_type=jnp.float32)
    cos_t = cos_t_ref[0]
    sin_t = sin_t_ref[0]
    gq = gq_ref[...]
    for hd in range(n_heads):
        blk = q_scr[hd * qk:(hd + 1) * qk, :]
        rinv = lax.rsqrt(jnp.mean(blk * blk, axis=0, keepdims=True) + EPS)
        qn = blk * rinv * gq
        r1 = qn[nope:nope + half]
        r2 = qn[nope + half:]
        q_out[0, hd, :nope, :] = qn[:nope].astype(jnp.bfloat16)
        q_out[0, hd, nope:nope + half, :] = (r1 * cos_t - r2 * sin_t).astype(jnp.bfloat16)
        q_out[0, hd, nope + half:, :] = (r2 * cos_t + r1 * sin_t).astype(jnp.bfloat16)

    v_t = lax.dot_general(wv_t_ref[...], ckv_n, _NT, preferred_element_type=jnp.float32)
    v_out[0, :, :v_dim, :] = v_t.astype(jnp.bfloat16).reshape(n_heads, v_dim, v_t.shape[-1])
    v_out[0, :, v_dim:, :] = jnp.ones((n_heads, ONES_ROWS, v_t.shape[-1]), jnp.bfloat16)

    k_scr[...] = jnp.dot(ckv_n, wk_ref[...], preferred_element_type=jnp.float32)
    ss_rope = jnp.sum(k_r * k_r, axis=-1, keepdims=True)
    rot = (k_r * gk_rope_ref[...]) * cos_ref[0] + (k_r_sw * gk_rope_sw_ref[...]) * sin_ref[0]
    gk_nope = gk_nope_ref[...]
    for hd in range(n_heads):
        blk = k_scr[:, hd * nope:(hd + 1) * nope]
        ss = jnp.sum(blk * blk, axis=-1, keepdims=True) + ss_rope
        rinv = lax.rsqrt(ss * (1.0 / qk) + EPS)
        k_out[0, hd, :, :nope] = (blk * rinv * gk_nope).astype(jnp.bfloat16)
        k_out[0, hd, :, nope:] = (rot * rinv).astype(jnp.bfloat16)


def _qkv(lat, tables, gq_lat, gkv_lat, wq_t, wk, wv_t, gq_b, gk_nope, gk_rope, gk_rope_sw,
         dims, bsz, s_len, tm):
    n_heads, q_lora, kv_lora, nope, rope, v_dim = dims
    qk = nope + rope
    half = rope // 2
    cos_t, sin_t, cos, sin = tables
    spb = s_len // tm
    n_lat = lat.shape[1]
    tok = lambda i: (i // spb, i % spb)
    return pl.pallas_call(
        functools.partial(_qkv_kernel, dims),
        grid=(bsz * spb,),
        in_specs=[
            pl.BlockSpec((tm, n_lat), lambda i: (i, 0)),
            pl.BlockSpec((1, half, tm), lambda i: (i // spb, 0, i % spb)),
            pl.BlockSpec((1, half, tm), lambda i: (i // spb, 0, i % spb)),
            pl.BlockSpec((1, tm, rope), lambda i: (i // spb, i % spb, 0)),
            pl.BlockSpec((1, tm, rope), lambda i: (i // spb, i % spb, 0)),
            _resident(gq_lat.shape),
            _resident(gkv_lat.shape),
            _resident(wq_t.shape),
            _resident(wk.shape),
            _resident(wv_t.shape),
            _resident(gq_b.shape),
            _resident(gk_nope.shape),
            _resident(gk_rope.shape),
            _resident(gk_rope_sw.shape),
        ],
        out_specs=[
            pl.BlockSpec((1, n_heads, qk, tm), lambda i: (i // spb, 0, 0, i % spb)),
            pl.BlockSpec((1, n_heads, tm, qk), lambda i: (i // spb, 0, i % spb, 0)),
            pl.BlockSpec((1, n_heads, v_dim + ONES_ROWS, tm), lambda i: (i // spb, 0, 0, i % spb)),
        ],
        out_shape=[
            jax.ShapeDtypeStruct((bsz, n_heads, qk, s_len), jnp.bfloat16),
            jax.ShapeDtypeStruct((bsz, n_heads, s_len, qk), jnp.bfloat16),
            jax.ShapeDtypeStruct((bsz, n_heads, v_dim + ONES_ROWS, s_len), jnp.bfloat16),
        ],
        scratch_shapes=[
            pltpu.VMEM((n_heads * qk, tm), jnp.float32),
            pltpu.VMEM((tm, n_heads * nope), jnp.float32),
        ],
        compiler_params=_params(("parallel",)),
        name="mla_qkv",
    )(lat, cos_t, sin_t, cos, sin, gq_lat, gkv_lat, wq_t, wk, wv_t,
      gq_b, gk_nope, gk_rope, gk_rope_sw)


def _attn_kernel(bq, bk, hps, v_dim, q_ref, k_ref, v_ref, sg_ref, o_ref,
                 s_ref, m_ref, alpha_ref, acc_ref):
    qi = pl.program_id(2)
    ratio = bq // bk
    first_diag = qi * ratio
    m_ref[...] = jnp.full(m_ref.shape, MASK_VALUE, jnp.float32)
    acc_ref[...] = jnp.zeros(acc_ref.shape, jnp.float32)

    def score_matmuls(j):
        start = pl.multiple_of(j * bk, bk)
        return [jnp.dot(k_ref[0, hh, pl.ds(start, bk), :], q_ref[0, hh],
                        preferred_element_type=jnp.float32) for hh in range(hps)]

    def finish_block(j):
        start = pl.multiple_of(j * bk, bk)
        for hh in range(hps):
            p = jnp.exp2(s_ref[hh] - m_ref[hh]).astype(jnp.bfloat16)
            v_blk = v_ref[0, hh, :, pl.ds(start, bk)]
            pv = jnp.dot(v_blk, p, preferred_element_type=jnp.float32)
            acc_ref[hh] = alpha_ref[hh] * acc_ref[hh] + pv

    def stage_block(j, scores, masked):
        for hh in range(hps):
            s_t = scores[hh]
            if masked:
                kv_pos = j * bk + lax.broadcasted_iota(jnp.int32, s_t.shape, 0)
                q_pos = qi * bq + lax.broadcasted_iota(jnp.int32, s_t.shape, 1)
                s_t = jnp.where(kv_pos <= q_pos, s_t, MASK_VALUE)
            m_prev = m_ref[hh]
            m_new = jnp.maximum(m_prev, jnp.max(s_t, axis=0, keepdims=True))
            s_ref[hh] = s_t
            alpha_ref[hh] = jnp.exp2(m_prev - m_new)
            m_ref[hh] = m_new

    def pipelined(j, masked):
        scores = score_matmuls(j)
        finish_block(j - 1)
        stage_block(j, scores, masked)

    stage_block(0, score_matmuls(0), True)

    def body(j, carry):
        pipelined(j, False)
        return carry

    lax.fori_loop(1, first_diag, body, 0)

    @pl.when(qi >= 1)
    def _():
        pipelined(first_diag, True)

    for t in range(1, ratio):
        pipelined(first_diag + t, True)

    finish_block(first_diag + ratio - 1)

    for hh in range(hps):
        acc = acc_ref[hh]
        o_t = acc[:v_dim] * (1.0 / acc[v_dim:v_dim + 1])
        sg = sg_ref[0, :, hh * v_dim:(hh + 1) * v_dim].astype(jnp.float32)
        o_ref[0, :, hh * v_dim:(hh + 1) * v_dim] = (o_t.T * sg).astype(jnp.bfloat16)


def _attention(q_t, k, v_t, sgate, bq, bk, hps):
    bsz, n_heads, qk, s_len = q_t.shape
    v_rows = v_t.shape[2]
    v_dim = v_rows - ONES_ROWS
    assert bq % bk == 0 and s_len % bq == 0 and n_heads % hps == 0
    return pl.pallas_call(
        functools.partial(_attn_kernel, bq, bk, hps, v_dim),
        grid=(bsz, n_heads // hps, s_len // bq),
        in_specs=[
            pl.BlockSpec((1, hps, qk, bq), lambda b, h, i: (b, h, 0, i)),
            pl.BlockSpec((1, hps, s_len, qk), lambda b, h, i: (b, h, 0, 0)),
            pl.BlockSpec((1, hps, v_rows, s_len), lambda b, h, i: (b, h, 0, 0)),
            pl.BlockSpec((1, bq, hps * v_dim), lambda b, h, i: (b, i, h)),
        ],
        out_specs=pl.BlockSpec((1, bq, hps * v_dim), lambda b, h, i: (b, i, h)),
        out_shape=jax.ShapeDtypeStruct((bsz, s_len, n_heads * v_dim), jnp.bfloat16),
        scratch_shapes=[
            pltpu.VMEM((hps, bk, bq), jnp.float32),
            pltpu.VMEM((hps, 1, bq), jnp.float32),
            pltpu.VMEM((hps, 1, bq), jnp.float32),
            pltpu.VMEM((hps, v_rows, bq), jnp.float32),
        ],
        compiler_params=_params(("parallel", "parallel", "arbitrary")),
        name="mla_attention",
    )(q_t, k, v_t, sgate)


def _out_ple_kernel(act_ref, h_ref, p_ref, w_out_ref, w_proj_ref, g_ref, w_gate_ref, b_ref, o_ref):
    y = jnp.dot(act_ref[...], w_out_ref[...], preferred_element_type=jnp.float32)
    h1 = h_ref[...] + y
    e = jnp.dot(p_ref[...].astype(jnp.bfloat16), w_proj_ref[...], preferred_element_type=jnp.float32)
    e = e * _rms_scale(e) * g_ref[...]
    gate = jnp.dot(h1.astype(jnp.bfloat16), w_gate_ref[...], preferred_element_type=jnp.float32)
    o_ref[...] = h1 + _sigmoid(gate + b_ref[...]) * e


def _out_ple(act, h2d, p2d, w_out, w_proj, g, w_gate, b, tm):
    n_tok, d = h2d.shape
    width = act.shape[1]
    ple = p2d.shape[1]
    return pl.pallas_call(
        _out_ple_kernel,
        grid=(n_tok // tm,),
        in_specs=[
            pl.BlockSpec((tm, width), lambda i: (i, 0)),
            pl.BlockSpec((tm, d), lambda i: (i, 0)),
            pl.BlockSpec((tm, ple), lambda i: (i, 0)),
            _resident(w_out.shape),
            _resident(w_proj.shape),
            _resident(g.shape),
            _resident(w_gate.shape),
            _resident(b.shape),
        ],
        out_specs=pl.BlockSpec((tm, d), lambda i: (i, 0)),
        out_shape=jax.ShapeDtypeStruct((n_tok, d), jnp.float32),
        compiler_params=_params(("parallel",)),
        name="out_proj_ple",
    )(act, h2d, p2d, w_out, w_proj, g, w_gate, b)


def _conv_in_kernel(tiles_per_seq, h_ref, g_ref, wb_ref, wc_ref, wh_ref, wg_ref, cw_ref,
                    o_ref, u_scr, carry_scr):
    i = pl.program_id(0)
    n = pl.program_id(1)

    @pl.when(n == 0)
    def _():
        h = h_ref[...]
        u_scr[...] = (h * _rms_scale(h) * g_ref[...]).astype(jnp.bfloat16)

    @pl.when(i % tiles_per_seq == 0)
    def _():
        carry_scr[n] = jnp.zeros(carry_scr.shape[1:], jnp.float32)

    u = u_scr[...]
    b_gate = jnp.dot(u, wb_ref[...], preferred_element_type=jnp.float32)
    c_gate = jnp.dot(u, wc_ref[...], preferred_element_type=jnp.float32)
    hid = jnp.dot(u, wh_ref[...], preferred_element_type=jnp.float32)
    gate = jnp.dot(u, wg_ref[...], preferred_element_type=jnp.float32)
    t = c_gate * hid
    tm = t.shape[0]
    carry = carry_scr[n]
    prev1 = carry[7:8, :]
    prev2 = carry[6:7, :]
    row = lax.broadcasted_iota(jnp.int32, t.shape, 0)
    t1 = jnp.where(row == 0, prev1, pltpu.roll(t, 1, 0))
    t2 = jnp.where(row == 0, prev2, jnp.where(row == 1, prev1, pltpu.roll(t, 2, 0)))
    carry_scr[n] = t[tm - 8:, :]
    cw = cw_ref[...]
    conv = cw[2:3, :] * t + cw[1:2, :] * t1 + cw[0:1, :] * t2
    o_ref[...] = (gate * _sigmoid(gate) * b_gate * conv).astype(jnp.bfloat16)


def _conv_in(h2d, g, w_in, conv_w, s_len, tm, tn):
    n_tok, d = h2d.shape
    width = w_in.shape[1] // 4
    nb = width // tn
    w_spec = lambda part: pl.BlockSpec((d, tn), lambda i, n: (0, part * nb + n))
    return pl.pallas_call(
        functools.partial(_conv_in_kernel, s_len // tm),
        grid=(n_tok // tm, nb),
        in_specs=[
            pl.BlockSpec((tm, d), lambda i, n: (i, 0)),
            pl.BlockSpec((1, d), lambda i, n: (0, 0)),
            w_spec(0), w_spec(1), w_spec(2), w_spec(3),
            pl.BlockSpec((conv_w.shape[0], tn), lambda i, n: (0, n)),
        ],
        out_specs=pl.BlockSpec((tm, tn), lambda i, n: (i, n)),
        out_shape=jax.ShapeDtypeStruct((n_tok, width), jnp.bfloat16),
        scratch_shapes=[
            pltpu.VMEM((tm, d), jnp.bfloat16),
            pltpu.VMEM((nb, 8, tn), jnp.float32),
        ],
        compiler_params=_params(("arbitrary", "arbitrary")),
        name="conv_in_proj",
    )(h2d, g, w_in, w_in, w_in, w_in, conv_w)


def kernel(x, p, positions, norm_g, a_w_in, a_q_norm_g, a_kv_norm_g, a_w_q_up, a_w_kv_up,
           a_qn_g, a_kn_g, a_w_out, b_w_in, b_conv_w, b_w_out, ple_w_proj, ple_norm_g,
           ple_w_gate, ple_b_gate):
    bsz, s_len, d = x.shape
    depth = p.shape[0]
    ple_dim = p.shape[-1]
    assert depth == 2 and a_w_in.shape[0] == 1 and b_w_in.shape[0] == 1
    q_lora = a_q_norm_g.shape[-1]
    kv_lora = a_kv_norm_g.shape[-1]
    qk = a_qn_g.shape[-1]
    att_width = a_w_out.shape[1]
    rope = a_w_in.shape[-1] - q_lora - kv_lora - att_width
    nope = qk - rope
    n_heads = a_w_q_up.shape[-1] // qk
    v_dim = a_w_kv_up.shape[-1] // n_heads - nope
    assert n_heads * v_dim == att_width
    half = rope // 2
    n_tok = bsz * s_len
    bf = jnp.bfloat16

    tm, bq, bk, tm_conv, tn_conv = _tile_plan(s_len, b_w_out.shape[1])
    assert s_len % tm == 0 and s_len % tm_conv == 0

    w_in0 = a_w_in[0]
    n_lat = q_lora + kv_lora
    w_rope = w_in0[:, n_lat:n_lat + rope]
    w_rope_sw = jnp.concatenate([w_rope[:, half:], w_rope[:, :half]], axis=1)
    w_lat = jnp.concatenate([w_in0[:, :n_lat], w_rope, w_rope_sw], axis=1).astype(bf)
    w_gate0 = w_in0[:, n_lat + rope:].astype(bf)
    wq_t = a_w_q_up[0].T.astype(bf)
    w_kv = a_w_kv_up[0].reshape(kv_lora, n_heads, nope + v_dim)
    wk = w_kv[:, :, :nope].reshape(kv_lora, n_heads * nope).astype(bf)
    wv_t = w_kv[:, :, nope:].reshape(kv_lora, n_heads * v_dim).T.astype(bf)
    gq_b = jnp.broadcast_to((a_qn_g[0] * (qk ** -0.5 * LOG2_E))[:, None], (qk, tm))
    gk = a_kn_g[0]
    gk_nope = gk[None, :nope]
    gk_rope = gk[None, nope:]
    gk_rope_sw = jnp.concatenate([gk[nope + half:], gk[nope:nope + half]])[None, :]

    h = x.reshape(n_tok, d)
    p2 = p.reshape(depth, n_tok, ple_dim)

    tables = _rope_tables(positions, rope)
    lat, sgate = _mla_in(h, norm_g[0][None, :], w_lat, w_gate0, tm)
    dims = (n_heads, q_lora, kv_lora, nope, rope, v_dim)
    q_t, k, v_t = _qkv(lat, tables, a_q_norm_g[0][None, :], a_kv_norm_g[0][None, :],
                       wq_t, wk, wv_t, gq_b, gk_nope, gk_rope, gk_rope_sw,
                       dims, bsz, s_len, tm)
    act = _attention(q_t, k, v_t, sgate.reshape(bsz, s_len, att_width), bq, bk, 2)
    h = _out_ple(act.reshape(n_tok, att_width), h, p2[0], a_w_out[0].astype(bf),
                 ple_w_proj[0].astype(bf), ple_norm_g[0][None, :], ple_w_gate[0].astype(bf),
                 ple_b_gate[0][None, :], tm)

    act = _conv_in(h, norm_g[1][None, :], b_w_in[0].astype(bf), b_conv_w[0], s_len,
                   tm_conv, tn_conv)
    h = _out_ple(act, h, p2[1], b_w_out[0].astype(bf), ple_w_proj[1].astype(bf),
                 ple_norm_g[1][None, :], ple_w_gate[1].astype(bf), ple_b_gate[1][None, :], tm)
    return h.reshape(bsz, s_len, d)
```

```python
import functools

import jax
import jax.numpy as jnp
from jax import lax
from jax.experimental import pallas as pl
from jax.experimental.pallas import tpu as pltpu

EPS = 1e-6
ROPE_THETA = 10000.0
MASK_VALUE = -1e30
LOG2_E = 1.4426950408889634
ONES_ROWS = 16
SCORE_LEAD = 1
VMEM_LIMIT_BYTES = 56 * 1024 * 1024

_NT = (((1,), (1,)), ((), ()))


def _tile_plan(s_len, conv_width):
    return min(512, s_len), min(1024, s_len), min(512, s_len), min(1024, s_len), min(512, conv_width)


def _params(semantics):
    return pltpu.CompilerParams(dimension_semantics=semantics,
                                vmem_limit_bytes=VMEM_LIMIT_BYTES)


def _resident(shape):
    nd = len(shape)
    return pl.BlockSpec(shape, lambda *_: (0,) * nd, pipeline_mode=pl.Buffered(1))


def _rms_scale(x):
    return lax.rsqrt(jnp.mean(x * x, axis=-1, keepdims=True) + EPS)


def _sigmoid(x):
    return 1.0 / (1.0 + jnp.exp(-x))


def _rope_kernel(pos_ref, invf_ref, cos_t_ref, sin_t_ref):
    ang_t = invf_ref[...] * pos_ref[0].astype(jnp.float32)
    cos_t_ref[0] = jnp.cos(ang_t)
    sin_t_ref[0] = jnp.sin(ang_t)


def _rope_tables(positions, rope_dim):
    bsz, s_len = positions.shape
    half = rope_dim // 2
    ts = min(s_len, 2048)
    inv_freq = ROPE_THETA ** (-jnp.arange(0, rope_dim, 2, dtype=jnp.float32) / rope_dim)
    table = pl.BlockSpec((1, half, ts), lambda b, i: (b, 0, i))
    return pl.pallas_call(
        _rope_kernel,
        grid=(bsz, s_len // ts),
        in_specs=[
            pl.BlockSpec((1, 1, ts), lambda b, i: (b, 0, i)),
            pl.BlockSpec((half, 1), lambda b, i: (0, 0)),
        ],
        out_specs=[table, table],
        out_shape=[jax.ShapeDtypeStruct((bsz, half, s_len), jnp.float32)] * 2,
        compiler_params=_params(("parallel", "parallel")),
        name="rope_tables",
    )(positions[:, None, :], inv_freq[:, None])


def _mla_in_kernel(h_ref, g_ref, w_lat_ref, w_gate_ref, lat_ref, sgate_ref):
    h = h_ref[...]
    u = (h * _rms_scale(h) * g_ref[...]).astype(jnp.bfloat16)
    lat_ref[...] = jnp.dot(u, w_lat_ref[...], preferred_element_type=jnp.float32)
    gate = jnp.dot(u, w_gate_ref[...], preferred_element_type=jnp.float32)
    sgate_ref[...] = (gate * _sigmoid(gate)).astype(jnp.bfloat16)


def _mla_in(h2d, g, w_lat, w_gate, tm):
    n_tok, d = h2d.shape
    n_lat, n_gate = w_lat.shape[1], w_gate.shape[1]
    return pl.pallas_call(
        _mla_in_kernel,
        grid=(n_tok // tm,),
        in_specs=[
            pl.BlockSpec((tm, d), lambda i: (i, 0)),
            _resident((1, d)),
            _resident((d, n_lat)),
            _resident((d, n_gate)),
        ],
        out_specs=[
            pl.BlockSpec((tm, n_lat), lambda i: (i, 0)),
            pl.BlockSpec((tm, n_gate), lambda i: (i, 0)),
        ],
        out_shape=[
            jax.ShapeDtypeStruct((n_tok, n_lat), jnp.float32),
            jax.ShapeDtypeStruct((n_tok, n_gate), jnp.bfloat16),
        ],
        compiler_params=_params(("parallel",)),
        name="mla_in_proj",
    )(h2d, g, w_lat, w_gate)


def _qkv_kernel(dims, lat_ref, cos_t_ref, sin_t_ref,
                gq_lat_ref, gkv_lat_ref, wq_t_ref, wk_ref, wv_t_ref,
                gq_ref, gk_nope_ref, gk_rope_ref,
                q_out, k_out, v_out, q_scr, k_scr):
    n_heads, q_lora, kv_lora, nope, rope, v_dim = dims
    qk = nope + rope
    half = rope // 2
    lat = lat_ref[...]
    c_q = lat[:, :q_lora]
    c_kv = lat[:, q_lora:q_lora + kv_lora]
    k_r = lat[:, q_lora + kv_lora:q_lora + kv_lora + rope]

    cq_n = (c_q * _rms_scale(c_q) * gq_lat_ref[...]).astype(jnp.bfloat16)
    ckv_n = (c_kv * _rms_scale(c_kv) * gkv_lat_ref[...]).astype(jnp.bfloat16)

    q_scr[...] = lax.dot_general(wq_t_ref[...], cq_n, _NT, preferred_element_type=jnp.float32)
    cos_t = cos_t_ref[0]
    sin_t = sin_t_ref[0]
    gq = gq_ref[...]
    for hd in range(n_heads):
        blk = q_scr[hd * qk:(hd + 1) * qk, :]
        rinv = lax.rsqrt(jnp.mean(blk * blk, axis=0, keepdims=True) + EPS)
        qn = blk * rinv * gq
        r1 = qn[nope:nope + half]
        r2 = qn[nope + half:]
        q_out[0, hd, :nope, :] = qn[:nope].astype(jnp.bfloat16)
        q_out[0, hd, nope:nope + half, :] = (r1 * cos_t - r2 * sin_t).astype(jnp.bfloat16)
        q_out[0, hd, nope + half:, :] = (r2 * cos_t + r1 * sin_t).astype(jnp.bfloat16)

    v_t = lax.dot_general(wv_t_ref[...], ckv_n, _NT, preferred_element_type=jnp.float32)
    v_out[0, :, :v_dim, :] = v_t.astype(jnp.bfloat16).reshape(n_heads, v_dim, v_t.shape[-1])
    v_out[0, :, v_dim:, :] = jnp.ones((n_heads, ONES_ROWS, v_t.shape[-1]), jnp.bfloat16)

    k_scr[...] = jnp.dot(ckv_n, wk_ref[...], preferred_element_type=jnp.float32)
    ss_rope = jnp.sum(k_r * k_r, axis=-1, keepdims=True)
    kr_t = k_r.T * gk_rope_ref[...]
    kr1 = kr_t[:half]
    kr2 = kr_t[half:]
    rot = jnp.concatenate([kr1 * cos_t - kr2 * sin_t, kr2 * cos_t + kr1 * sin_t], axis=0).T
    gk_nope = gk_nope_ref[...]
    for hd in range(n_heads):
        blk = k_scr[:, hd * nope:(hd + 1) * nope]
        ss = jnp.sum(blk * blk, axis=-1, keepdims=True) + ss_rope
        rinv = lax.rsqrt(ss * (1.0 / qk) + EPS)
        k_out[0, hd, :, :nope] = (blk * rinv * gk_nope).astype(jnp.bfloat16)
        k_out[0, hd, :, nope:] = (rot * rinv).astype(jnp.bfloat16)


def _qkv(lat, tables, gq_lat, gkv_lat, wq_t, wk, wv_t, gq_b, gk_nope, gk_rope_b,
         dims, bsz, s_len, tm):
    n_heads, q_lora, kv_lora, nope, rope, v_dim = dims
    qk = nope + rope
    half = rope // 2
    cos_t, sin_t = tables
    spb = s_len // tm
    n_lat = lat.shape[1]
    tok = lambda i: (i // spb, i % spb)
    return pl.pallas_call(
        functools.partial(_qkv_kernel, dims),
        grid=(bsz * spb,),
        in_specs=[
            pl.BlockSpec((tm, n_lat), lambda i: (i, 0)),
            pl.BlockSpec((1, half, tm), lambda i: (i // spb, 0, i % spb)),
            pl.BlockSpec((1, half, tm), lambda i: (i // spb, 0, i % spb)),
            _resident(gq_lat.shape),
            _resident(gkv_lat.shape),
            _resident(wq_t.shape),
            _resident(wk.shape),
            _resident(wv_t.shape),
            _resident(gq_b.shape),
            _resident(gk_nope.shape),
            _resident(gk_rope_b.shape),
        ],
        out_specs=[
            pl.BlockSpec((1, n_heads, qk, tm), lambda i: (i // spb, 0, 0, i % spb)),
            pl.BlockSpec((1, n_heads, tm, qk), lambda i: (i // spb, 0, i % spb, 0)),
            pl.BlockSpec((1, n_heads, v_dim + ONES_ROWS, tm), lambda i: (i // spb, 0, 0, i % spb)),
        ],
        out_shape=[
            jax.ShapeDtypeStruct((bsz, n_heads, qk, s_len), jnp.bfloat16),
            jax.ShapeDtypeStruct((bsz, n_heads, s_len, qk), jnp.bfloat16),
            jax.ShapeDtypeStruct((bsz, n_heads, v_dim + ONES_ROWS, s_len), jnp.bfloat16),
        ],
        scratch_shapes=[
            pltpu.VMEM((n_heads * qk, tm), jnp.float32),
            pltpu.VMEM((tm, n_heads * nope), jnp.float32),
        ],
        compiler_params=_params(("parallel",)),
        name="mla_qkv",
    )(lat, cos_t, sin_t, gq_lat, gkv_lat, wq_t, wk, wv_t, gq_b, gk_nope, gk_rope_b)


def _attn_kernel(bq, bk, hps, v_dim, q_ref, k_ref, v_ref, sg_ref, o_ref,
                 s_ref, m_ref, alpha_ref, acc_ref):
    qi = pl.program_id(2)
    ncol = bq // bk
    first_diag = qi * ncol
    nslots = s_ref.shape[0]
    m_ref[...] = jnp.full(m_ref.shape, MASK_VALUE, jnp.float32)
    acc_ref[...] = jnp.zeros(acc_ref.shape, jnp.float32)

    def chains(t):
        return [(hh, c) for hh in range(hps) for c in range(ncol) if t is None or c >= t]

    def score_matmul(j, hh, c):
        start = pl.multiple_of(j * bk, bk)
        return jnp.dot(k_ref[0, hh, pl.ds(start, bk), :], q_ref[0, hh, :, c * bk:(c + 1) * bk],
                       preferred_element_type=jnp.float32)

    def finish_chain(j, hh, c, slot):
        start = pl.multiple_of(j * bk, bk)
        i = hh * ncol + c
        p = jnp.exp2(s_ref[slot, i] - m_ref[i]).astype(jnp.bfloat16)
        v_blk = v_ref[0, hh, :, pl.ds(start, bk)]
        pv = jnp.dot(v_blk, p, preferred_element_type=jnp.float32)
        acc_ref[i] = alpha_ref[i] * acc_ref[i] + pv

    def score_matmuls(j, t):
        return {(hh, c): score_matmul(j, hh, c) for hh, c in chains(t)}

    def finish_block(j, t, slot):
        for hh, c in chains(t):
            finish_chain(j, hh, c, slot)

    def stage_block(scores, mask_of, slot):
        for (hh, c), s_t in scores.items():
            i = hh * ncol + c
            mask = mask_of(c)
            if mask is not None:
                s_t = jnp.where(mask, s_t, MASK_VALUE)
            m_prev = m_ref[i]
            m_new = jnp.maximum(m_prev, jnp.max(s_t, axis=0, keepdims=True))
            s_ref[slot, i] = s_t
            alpha_ref[i] = jnp.exp2(m_prev - m_new)
            m_ref[i] = m_new

    row = lax.broadcasted_iota(jnp.int32, (bk, bk), 0)
    col = lax.broadcasted_iota(jnp.int32, (bk, bk), 1)

    def pipelined(j, t, slot):
        todo_s = chains(t)
        todo_f = chains(None if not t else t - 1)
        scores = {}
        n_s = n_f = 0
        while n_s < len(todo_s) or n_f < len(todo_f):
            if n_s < len(todo_s) and (n_s < n_f + SCORE_LEAD or n_f == len(todo_f)):
                scores[todo_s[n_s]] = score_matmul(j, *todo_s[n_s])
                n_s += 1
            else:
                finish_chain(j - 1, *todo_f[n_f], (slot - 1) % nslots)
                n_f += 1
        stage_block(scores, lambda c: (row <= col) if c == t else None, slot)

    stage_block(score_matmuls(0, None),
                lambda c: (row <= qi * bq + col) if c == 0 else None, 0)

    if nslots == 1:
        def body(j, carry):
            pipelined(j, None, 0)
            return carry

        lax.fori_loop(1, first_diag, body, 0)
    else:
        def body(u, carry):
            pipelined(2 * u - 1, None, 1)
            pipelined(2 * u, None, 0)
            return carry

        lax.fori_loop(1, first_diag // 2, body, 0)

    @pl.when(qi >= 1)
    def _():
        if nslots == 2:
            pipelined(first_diag - 1, None, 1)
        pipelined(first_diag, 0, 0)

    for t in range(1, ncol):
        pipelined(first_diag + t, t, t % nslots)

    finish_block(first_diag + ncol - 1, ncol - 1, (ncol - 1) % nslots)

    for hh in range(hps):
        sg = sg_ref[0, :, hh * v_dim:(hh + 1) * v_dim].astype(jnp.float32)
        for c in range(ncol):
            acc = acc_ref[hh * ncol + c]
            o_t = acc[:v_dim] * (1.0 / acc[v_dim:v_dim + 1])
            o_ref[0, c * bk:(c + 1) * bk, hh * v_dim:(hh + 1) * v_dim] = (
                o_t.T * sg[c * bk:(c + 1) * bk]).astype(jnp.bfloat16)


def _attention(q_t, k, v_t, sgate, bq, bk, hps):
    bsz, n_heads, qk, s_len = q_t.shape
    v_rows = v_t.shape[2]
    v_dim = v_rows - ONES_ROWS
    assert bq % bk == 0 and s_len % bq == 0 and n_heads % hps == 0
    n_chains = hps * (bq // bk)
    nslots = 2 if (bq // bk) % 2 == 0 else 1
    return pl.pallas_call(
        functools.partial(_attn_kernel, bq, bk, hps, v_dim),
        grid=(bsz, n_heads // hps, s_len // bq),
        in_specs=[
            pl.BlockSpec((1, hps, qk, bq), lambda b, h, i: (b, h, 0, i)),
            pl.BlockSpec((1, hps, s_len, qk), lambda b, h, i: (b, h, 0, 0)),
            pl.BlockSpec((1, hps, v_rows, s_len), lambda b, h, i: (b, h, 0, 0)),
            pl.BlockSpec((1, bq, hps * v_dim), lambda b, h, i: (b, i, h)),
        ],
        out_specs=pl.BlockSpec((1, bq, hps * v_dim), lambda b, h, i: (b, i, h)),
        out_shape=jax.ShapeDtypeStruct((bsz, s_len, n_heads * v_dim), jnp.bfloat16),
        scratch_shapes=[
            pltpu.VMEM((nslots, n_chains, bk, bk), jnp.float32),
            pltpu.VMEM((n_chains, 1, bk), jnp.float32),
            pltpu.VMEM((n_chains, 1, bk), jnp.float32),
            pltpu.VMEM((n_chains, v_rows, bk), jnp.float32),
        ],
        compiler_params=_params(("parallel", "parallel", "arbitrary")),
        name="mla_attention",
    )(q_t, k, v_t, sgate)


def _out_ple_kernel(act_ref, h_ref, p_ref, w_out_ref, w_proj_ref, g_ref, w_gate_ref, b_ref, o_ref):
    y = jnp.dot(act_ref[...], w_out_ref[...], preferred_element_type=jnp.float32)
    h1 = h_ref[...] + y
    e = jnp.dot(p_ref[0].astype(jnp.bfloat16), w_proj_ref[...], preferred_element_type=jnp.float32)
    e = e * _rms_scale(e) * g_ref[...]
    gate = jnp.dot(h1.astype(jnp.bfloat16), w_gate_ref[...], preferred_element_type=jnp.float32)
    o_ref[...] = h1 + _sigmoid(gate + b_ref[...]) * e


def _out_ple(act, h2d, p_all, layer, w_out, w_proj, g, w_gate, b, tm):
    n_tok, d = h2d.shape
    width = act.shape[1]
    ple = p_all.shape[-1]
    return pl.pallas_call(
        _out_ple_kernel,
        grid=(n_tok // tm,),
        in_specs=[
            pl.BlockSpec((tm, width), lambda i: (i, 0)),
            pl.BlockSpec((tm, d), lambda i: (i, 0)),
            pl.BlockSpec((1, tm, ple), lambda i: (layer, i, 0)),
            _resident(w_out.shape),
            _resident(w_proj.shape),
            _resident(g.shape),
            _resident(w_gate.shape),
            _resident(b.shape),
        ],
        out_specs=pl.BlockSpec((tm, d), lambda i: (i, 0)),
        out_shape=jax.ShapeDtypeStruct((n_tok, d), jnp.float32),
        compiler_params=_params(("parallel",)),
        name="out_proj_ple",
    )(act, h2d, p_all, w_out, w_proj, g, w_gate, b)


def _conv_in_kernel(tiles_per_seq, h_ref, g_ref, wb_ref, wc_ref, wh_ref, wg_ref, cw_ref,
                    o_ref, u_scr, carry_scr):
    i = pl.program_id(0)
    n = pl.program_id(1)

    @pl.when(n == 0)
    def _():
        h = h_ref[...]
        u_scr[...] = (h * _rms_scale(h) * g_ref[...]).astype(jnp.bfloat16)

    @pl.when(i % tiles_per_seq == 0)
    def _():
        carry_scr[n] = jnp.zeros(carry_scr.shape[1:], jnp.float32)

    u = u_scr[...]
    b_gate = jnp.dot(u, wb_ref[...], preferred_element_type=jnp.float32)
    c_gate = jnp.dot(u, wc_ref[...], preferred_element_type=jnp.float32)
    hid = jnp.dot(u, wh_ref[...], preferred_element_type=jnp.float32)
    gate = jnp.dot(u, wg_ref[...], preferred_element_type=jnp.float32)
    t = c_gate * hid
    tm = t.shape[0]
    carry = carry_scr[n]
    prev1 = carry[7:8, :]
    prev2 = carry[6:7, :]
    row = lax.broadcasted_iota(jnp.int32, t.shape, 0)
    t1 = jnp.where(row == 0, prev1, pltpu.roll(t, 1, 0))
    t2 = jnp.where(row == 0, prev2, jnp.where(row == 1, prev1, pltpu.roll(t, 2, 0)))
    carry_scr[n] = t[tm - 8:, :]
    cw = cw_ref[...]
    conv = cw[2:3, :] * t + cw[1:2, :] * t1 + cw[0:1, :] * t2
    o_ref[...] = (gate * _sigmoid(gate) * b_gate * conv).astype(jnp.bfloat16)


def _conv_in(h2d, g, w_in, conv_w, s_len, tm, tn):
    n_tok, d = h2d.shape
    width = w_in.shape[1] // 4
    nb = width // tn
    w_spec = lambda part: pl.BlockSpec((d, tn), lambda i, n: (0, part * nb + n))
    return pl.pallas_call(
        functools.partial(_conv_in_kernel, s_len // tm),
        grid=(n_tok // tm, nb),
        in_specs=[
            pl.BlockSpec((tm, d), lambda i, n: (i, 0)),
            pl.BlockSpec((1, d), lambda i, n: (0, 0)),
            w_spec(0), w_spec(1), w_spec(2), w_spec(3),
            pl.BlockSpec((conv_w.shape[0], tn), lambda i, n: (0, n)),
        ],
        out_specs=pl.BlockSpec((tm, tn), lambda i, n: (i, n)),
        out_shape=jax.ShapeDtypeStruct((n_tok, width), jnp.bfloat16),
        scratch_shapes=[
            pltpu.VMEM((tm, d), jnp.bfloat16),
            pltpu.VMEM((nb, 8, tn), jnp.float32),
        ],
        compiler_params=_params(("arbitrary", "arbitrary")),
        name="conv_in_proj",
    )(h2d, g, w_in, w_in, w_in, w_in, conv_w)


def kernel(x, p, positions, norm_g, a_w_in, a_q_norm_g, a_kv_norm_g, a_w_q_up, a_w_kv_up,
           a_qn_g, a_kn_g, a_w_out, b_w_in, b_conv_w, b_w_out, ple_w_proj, ple_norm_g,
           ple_w_gate, ple_b_gate):
    bsz, s_len, d = x.shape
    depth = p.shape[0]
    ple_dim = p.shape[-1]
    assert depth == 2 and a_w_in.shape[0] == 1 and b_w_in.shape[0] == 1
    q_lora = a_q_norm_g.shape[-1]
    kv_lora = a_kv_norm_g.shape[-1]
    qk = a_qn_g.shape[-1]
    att_width = a_w_out.shape[1]
    rope = a_w_in.shape[-1] - q_lora - kv_lora - att_width
    nope = qk - rope
    n_heads = a_w_q_up.shape[-1] // qk
    v_dim = a_w_kv_up.shape[-1] // n_heads - nope
    assert n_heads * v_dim == att_width
    half = rope // 2
    n_tok = bsz * s_len
    bf = jnp.bfloat16

    tm, bq, bk, tm_conv, tn_conv = _tile_plan(s_len, b_w_out.shape[1])
    assert s_len % tm == 0 and s_len % tm_conv == 0

    w_in0 = a_w_in[0]
    n_lat = q_lora + kv_lora
    w_lat = w_in0[:, :n_lat + rope].astype(bf)
    w_gate0 = w_in0[:, n_lat + rope:].astype(bf)
    wq_t = a_w_q_up[0].T.astype(bf)
    w_kv = a_w_kv_up[0].reshape(kv_lora, n_heads, nope + v_dim)
    wk = w_kv[:, :, :nope].reshape(kv_lora, n_heads * nope).astype(bf)
    wv_t = w_kv[:, :, nope:].reshape(kv_lora, n_heads * v_dim).T.astype(bf)
    gq_b = jnp.broadcast_to((a_qn_g[0] * (qk ** -0.5 * LOG2_E))[:, None], (qk, tm))
    gk = a_kn_g[0]
    gk_nope = gk[None, :nope]
    gk_rope_b = jnp.broadcast_to(gk[nope:, None], (rope, tm))

    h = x.reshape(n_tok, d)
    p2 = p.reshape(depth, n_tok, ple_dim)

    tables = _rope_tables(positions, rope)
    lat, sgate = _mla_in(h, norm_g[0][None, :], w_lat, w_gate0, tm)
    dims = (n_heads, q_lora, kv_lora, nope, rope, v_dim)
    q_t, k, v_t = _qkv(lat, tables, a_q_norm_g[0][None, :], a_kv_norm_g[0][None, :],
                       wq_t, wk, wv_t, gq_b, gk_nope, gk_rope_b,
                       dims, bsz, s_len, tm)
    act = _attention(q_t, k, v_t, sgate.reshape(bsz, s_len, att_width), bq, bk, 2)
    h = _out_ple(act.reshape(n_tok, att_width), h, p2, 0, a_w_out[0].astype(bf),
                 ple_w_proj[0].astype(bf), ple_norm_g[0][None, :], ple_w_gate[0].astype(bf),
                 ple_b_gate[0][None, :], tm)

    act = _conv_in(h, norm_g[1][None, :], b_w_in[0].astype(bf), b_conv_w[0], s_len,
                   tm_conv, tn_conv)
    h = _out_ple(act, h, p2, 1, b_w_out[0].astype(bf), ple_w_proj[1].astype(bf),
                 ple_norm_g[1][None, :], ple_w_gate[1].astype(bf), ple_b_gate[1][None, :], tm)
    return h.reshape(bsz, s_len, d)
```

```python
import functools

import jax
import jax.numpy as jnp
from jax import lax
from jax.experimental import pallas as pl
from jax.experimental.pallas import tpu as pltpu

EPS = 1e-6
ROPE_THETA = 10000.0
MASK_VALUE = -1e30
LOG2_E = 1.4426950408889634
ONES_ROWS = 16
SCORE_LEAD = 1
VMEM_LIMIT_BYTES = 56 * 1024 * 1024

_NT = (((1,), (1,)), ((), ()))


def _tile_plan(s_len, conv_width):
    return (min(512, s_len), min(256, s_len), min(2048, s_len), min(512, s_len),
            min(1024, s_len), min(512, conv_width))


def _params(semantics):
    return pltpu.CompilerParams(dimension_semantics=semantics,
                                vmem_limit_bytes=VMEM_LIMIT_BYTES)


def _resident(shape):
    nd = len(shape)
    return pl.BlockSpec(shape, lambda *_: (0,) * nd, pipeline_mode=pl.Buffered(1))


def _rms_scale(x):
    return lax.rsqrt(jnp.mean(x * x, axis=-1, keepdims=True) + EPS)


def _sigmoid(x):
    return 1.0 / (1.0 + jnp.exp(-x))


def _rope_kernel(pos_ref, invf_ref, cos_t_ref, sin_t_ref):
    ang_t = invf_ref[...] * pos_ref[0].astype(jnp.float32)
    cos_t_ref[0] = jnp.cos(ang_t)
    sin_t_ref[0] = jnp.sin(ang_t)


def _rope_tables(positions, rope_dim):
    bsz, s_len = positions.shape
    half = rope_dim // 2
    ts = min(s_len, 2048)
    inv_freq = ROPE_THETA ** (-jnp.arange(0, rope_dim, 2, dtype=jnp.float32) / rope_dim)
    table = pl.BlockSpec((1, half, ts), lambda b, i: (b, 0, i))
    return pl.pallas_call(
        _rope_kernel,
        grid=(bsz, s_len // ts),
        in_specs=[
            pl.BlockSpec((1, 1, ts), lambda b, i: (b, 0, i)),
            pl.BlockSpec((half, 1), lambda b, i: (0, 0)),
        ],
        out_specs=[table, table],
        out_shape=[jax.ShapeDtypeStruct((bsz, half, s_len), jnp.float32)] * 2,
        compiler_params=_params(("parallel", "parallel")),
        name="rope_tables",
    )(positions[:, None, :], inv_freq[:, None])


def _mla_proj_kernel(dims, h_ref, g_ref, w_lat_ref, w_gate_ref, cos_t_ref, sin_t_ref,
                     gq_lat_ref, gkv_lat_ref, wq_t_ref, wk_ref, wv_t_ref,
                     gq_ref, gk_nope_ref, gk_rope_ref,
                     sgate_out, q_out, k_out, v_out, q_scr, k_scr):
    n_heads, q_lora, kv_lora, nope, rope, v_dim = dims
    qk = nope + rope
    half = rope // 2
    h = h_ref[...]
    u = (h * _rms_scale(h) * g_ref[...]).astype(jnp.bfloat16)
    lat = jnp.dot(u, w_lat_ref[...], preferred_element_type=jnp.float32)
    c_q = lat[:, :q_lora]
    c_kv = lat[:, q_lora:q_lora + kv_lora]
    k_r = lat[:, q_lora + kv_lora:q_lora + kv_lora + rope]
    cq_n = (c_q * _rms_scale(c_q) * gq_lat_ref[...]).astype(jnp.bfloat16)
    ckv_n = (c_kv * _rms_scale(c_kv) * gkv_lat_ref[...]).astype(jnp.bfloat16)

    k_scr[...] = jnp.dot(ckv_n, wk_ref[...], preferred_element_type=jnp.float32)
    q_scr[...] = lax.dot_general(wq_t_ref[...], cq_n, _NT, preferred_element_type=jnp.float32)
    v_t = lax.dot_general(wv_t_ref[...], ckv_n, _NT, preferred_element_type=jnp.float32)
    gate = jnp.dot(u, w_gate_ref[...], preferred_element_type=jnp.float32)
    cos_t = cos_t_ref[0]
    sin_t = sin_t_ref[0]

    ss_rope = jnp.sum(k_r * k_r, axis=-1, keepdims=True)
    kr_t = k_r.T * gk_rope_ref[...]
    kr1 = kr_t[:half]
    kr2 = kr_t[half:]
    rot = jnp.concatenate([kr1 * cos_t - kr2 * sin_t, kr2 * cos_t + kr1 * sin_t], axis=0).T
    gk_nope = gk_nope_ref[...]
    for hd in range(n_heads):
        blk = k_scr[:, hd * nope:(hd + 1) * nope]
        ss = jnp.sum(blk * blk, axis=-1, keepdims=True) + ss_rope
        rinv = lax.rsqrt(ss * (1.0 / qk) + EPS)
        k_out[0, hd, :, :nope] = (blk * rinv * gk_nope).astype(jnp.bfloat16)
        k_out[0, hd, :, nope:] = (rot * rinv).astype(jnp.bfloat16)

    gq = gq_ref[...]
    for hd in range(n_heads):
        blk = q_scr[hd * qk:(hd + 1) * qk, :]
        rinv = lax.rsqrt(jnp.mean(blk * blk, axis=0, keepdims=True) + EPS)
        qn = blk * rinv * gq
        r1 = qn[nope:nope + half]
        r2 = qn[nope + half:]
        q_out[0, hd, :nope, :] = qn[:nope].astype(jnp.bfloat16)
        q_out[0, hd, nope:nope + half, :] = (r1 * cos_t - r2 * sin_t).astype(jnp.bfloat16)
        q_out[0, hd, nope + half:, :] = (r2 * cos_t + r1 * sin_t).astype(jnp.bfloat16)

    v_out[0, :, :v_dim, :] = v_t.astype(jnp.bfloat16).reshape(n_heads, v_dim, v_t.shape[-1])
    v_out[0, :, v_dim:, :] = jnp.ones((n_heads, ONES_ROWS, v_t.shape[-1]), jnp.bfloat16)
    sgate_out[...] = (gate * _sigmoid(gate)).astype(jnp.bfloat16)


def _mla_proj(h2d, g, w_lat, w_gate, tables, gq_lat, gkv_lat, wq_t, wk, wv_t,
              gq_b, gk_nope, gk_rope_b, dims, bsz, s_len, tm):
    n_heads, q_lora, kv_lora, nope, rope, v_dim = dims
    n_tok, d = h2d.shape
    n_gate = w_gate.shape[1]
    qk = nope + rope
    half = rope // 2
    cos_t, sin_t = tables
    spb = s_len // tm
    table = pl.BlockSpec((1, half, tm), lambda i: (i // spb, 0, i % spb))
    weights = (g, w_lat, w_gate)
    consts = (gq_lat, gkv_lat, wq_t, wk, wv_t, gq_b, gk_nope, gk_rope_b)
    return pl.pallas_call(
        functools.partial(_mla_proj_kernel, dims),
        grid=(bsz * spb,),
        in_specs=([pl.BlockSpec((tm, d), lambda i: (i, 0))]
                  + [_resident(a.shape) for a in weights]
                  + [table, table]
                  + [_resident(a.shape) for a in consts]),
        out_specs=[
            pl.BlockSpec((tm, n_gate), lambda i: (i, 0)),
            pl.BlockSpec((1, n_heads, qk, tm), lambda i: (i // spb, 0, 0, i % spb)),
            pl.BlockSpec((1, n_heads, tm, qk), lambda i: (i // spb, 0, i % spb, 0)),
            pl.BlockSpec((1, n_heads, v_dim + ONES_ROWS, tm), lambda i: (i // spb, 0, 0, i % spb)),
        ],
        out_shape=[
            jax.ShapeDtypeStruct((n_tok, n_gate), jnp.bfloat16),
            jax.ShapeDtypeStruct((bsz, n_heads, qk, s_len), jnp.bfloat16),
            jax.ShapeDtypeStruct((bsz, n_heads, s_len, qk), jnp.bfloat16),
            jax.ShapeDtypeStruct((bsz, n_heads, v_dim + ONES_ROWS, s_len), jnp.bfloat16),
        ],
        scratch_shapes=[
            pltpu.VMEM((n_heads * qk, tm), jnp.float32),
            pltpu.VMEM((tm, n_heads * nope), jnp.float32),
        ],
        compiler_params=_params(("parallel",)),
        name="mla_proj",
    )(h2d, *weights, cos_t, sin_t, *consts)


def _attn_kernel(bq, bk, hps, v_dim, q_ref, k_ref, v_ref, sg_ref, o_ref,
                 s_ref, m_ref, alpha_ref, acc_ref):
    qi = pl.program_id(2)
    ncol = bq // bk
    first_diag = qi * ncol
    nslots = s_ref.shape[0]
    m_ref[...] = jnp.full(m_ref.shape, MASK_VALUE, jnp.float32)
    acc_ref[...] = jnp.zeros(acc_ref.shape, jnp.float32)

    def chains(t):
        return [(hh, c) for hh in range(hps) for c in range(ncol) if t is None or c >= t]

    def score_matmul(j, hh, c):
        start = pl.multiple_of(j * bk, bk)
        return jnp.dot(k_ref[0, hh, pl.ds(start, bk), :], q_ref[0, hh, :, c * bk:(c + 1) * bk],
                       preferred_element_type=jnp.float32)

    def finish_chain(j, hh, c, slot):
        start = pl.multiple_of(j * bk, bk)
        i = hh * ncol + c
        p = jnp.exp2(s_ref[slot, i] - m_ref[i]).astype(jnp.bfloat16)
        v_blk = v_ref[0, hh, :, pl.ds(start, bk)]
        pv = jnp.dot(v_blk, p, preferred_element_type=jnp.float32)
        acc_ref[i] = alpha_ref[i] * acc_ref[i] + pv

    def score_matmuls(j, t):
        return {(hh, c): score_matmul(j, hh, c) for hh, c in chains(t)}

    def finish_block(j, t, slot):
        for hh, c in chains(t):
            finish_chain(j, hh, c, slot)

    def stage_block(scores, mask_of, slot):
        for (hh, c), s_t in scores.items():
            i = hh * ncol + c
            mask = mask_of(c)
            if mask is not None:
                s_t = jnp.where(mask, s_t, MASK_VALUE)
            m_prev = m_ref[i]
            m_new = jnp.maximum(m_prev, jnp.max(s_t, axis=0, keepdims=True))
            s_ref[slot, i] = s_t
            alpha_ref[i] = jnp.exp2(m_prev - m_new)
            m_ref[i] = m_new

    row = lax.broadcasted_iota(jnp.int32, (bk, bk), 0)
    col = lax.broadcasted_iota(jnp.int32, (bk, bk), 1)

    def pipelined(j, t, slot):
        todo_s = chains(t)
        todo_f = chains(None if not t else t - 1)
        scores = {}
        n_s = n_f = 0
        while n_s < len(todo_s) or n_f < len(todo_f):
            if n_s < len(todo_s) and (n_s < n_f + SCORE_LEAD or n_f == len(todo_f)):
                scores[todo_s[n_s]] = score_matmul(j, *todo_s[n_s])
                n_s += 1
            else:
                finish_chain(j - 1, *todo_f[n_f], (slot - 1) % nslots)
                n_f += 1
        stage_block(scores, lambda c: (row <= col) if c == t else None, slot)

    stage_block(score_matmuls(0, None),
                lambda c: (row <= qi * bq + col) if c == 0 else None, 0)

    if nslots == 1:
        def body(j, carry):
            pipelined(j, None, 0)
            return carry

        lax.fori_loop(1, first_diag, body, 0)
    else:
        def body(u, carry):
            pipelined(2 * u - 1, None, 1)
            pipelined(2 * u, None, 0)
            return carry

        lax.fori_loop(1, first_diag // 2, body, 0)

    @pl.when(qi >= 1)
    def _():
        if nslots == 2:
            pipelined(first_diag - 1, None, 1)
        pipelined(first_diag, 0, 0)

    for t in range(1, ncol):
        pipelined(first_diag + t, t, t % nslots)

    finish_block(first_diag + ncol - 1, ncol - 1, (ncol - 1) % nslots)

    for hh in range(hps):
        sg = sg_ref[0, :, hh * v_dim:(hh + 1) * v_dim].astype(jnp.float32)
        for c in range(ncol):
            acc = acc_ref[hh * ncol + c]
            o_t = acc[:v_dim] * (1.0 / acc[v_dim:v_dim + 1])
            o_ref[0, c * bk:(c + 1) * bk, hh * v_dim:(hh + 1) * v_dim] = (
                o_t.T * sg[c * bk:(c + 1) * bk]).astype(jnp.bfloat16)


def _attention(q_t, k, v_t, sgate, bq, bk, hps):
    bsz, n_heads, qk, s_len = q_t.shape
    v_rows = v_t.shape[2]
    v_dim = v_rows - ONES_ROWS
    assert bq % bk == 0 and s_len % bq == 0 and n_heads % hps == 0
    n_chains = hps * (bq // bk)
    nslots = 2 if (bq // bk) % 2 == 0 else 1
    return pl.pallas_call(
        functools.partial(_attn_kernel, bq, bk, hps, v_dim),
        grid=(bsz, n_heads // hps, s_len // bq),
        in_specs=[
            pl.BlockSpec((1, hps, qk, bq), lambda b, h, i: (b, h, 0, i)),
            pl.BlockSpec((1, hps, s_len, qk), lambda b, h, i: (b, h, 0, 0)),
            pl.BlockSpec((1, hps, v_rows, s_len), lambda b, h, i: (b, h, 0, 0)),
            pl.BlockSpec((1, bq, hps * v_dim), lambda b, h, i: (b, i, h)),
        ],
        out_specs=pl.BlockSpec((1, bq, hps * v_dim), lambda b, h, i: (b, i, h)),
        out_shape=jax.ShapeDtypeStruct((bsz, s_len, n_heads * v_dim), jnp.bfloat16),
        scratch_shapes=[
            pltpu.VMEM((nslots, n_chains, bk, bk), jnp.float32),
            pltpu.VMEM((n_chains, 1, bk), jnp.float32),
            pltpu.VMEM((n_chains, 1, bk), jnp.float32),
            pltpu.VMEM((n_chains, v_rows, bk), jnp.float32),
        ],
        compiler_params=_params(("parallel", "parallel", "arbitrary")),
        name="mla_attention",
    )(q_t, k, v_t, sgate)


def _out_ple_kernel(act_ref, h_ref, p_ref, w_out_ref, w_proj_ref, g_ref, w_gate_ref, b_ref, o_ref):
    y = jnp.dot(act_ref[...], w_out_ref[...], preferred_element_type=jnp.float32)
    h1 = h_ref[...] + y
    e = jnp.dot(p_ref[0].astype(jnp.bfloat16), w_proj_ref[...], preferred_element_type=jnp.float32)
    e = e * _rms_scale(e) * g_ref[...]
    gate = jnp.dot(h1.astype(jnp.bfloat16), w_gate_ref[...], preferred_element_type=jnp.float32)
    o_ref[...] = h1 + _sigmoid(gate + b_ref[...]) * e


def _out_ple(act, h2d, p_all, layer, w_out, w_proj, g, w_gate, b, tm):
    n_tok, d = h2d.shape
    width = act.shape[1]
    ple = p_all.shape[-1]
    return pl.pallas_call(
        _out_ple_kernel,
        grid=(n_tok // tm,),
        in_specs=[
            pl.BlockSpec((tm, width), lambda i: (i, 0)),
            pl.BlockSpec((tm, d), lambda i: (i, 0)),
            pl.BlockSpec((1, tm, ple), lambda i: (layer, i, 0)),
            _resident(w_out.shape),
            _resident(w_proj.shape),
            _resident(g.shape),
            _resident(w_gate.shape),
            _resident(b.shape),
        ],
        out_specs=pl.BlockSpec((tm, d), lambda i: (i, 0)),
        out_shape=jax.ShapeDtypeStruct((n_tok, d), jnp.float32),
        compiler_params=_params(("parallel",)),
        name="out_proj_ple",
    )(act, h2d, p_all, w_out, w_proj, g, w_gate, b)


def _conv_in_kernel(tiles_per_seq, h_ref, g_ref, wb_ref, wc_ref, wh_ref, wg_ref, cw_ref,
                    o_ref, u_scr, carry_scr):
    i = pl.program_id(0)
    n = pl.program_id(1)

    @pl.when(n == 0)
    def _():
        h = h_ref[...]
        u_scr[...] = (h * _rms_scale(h) * g_ref[...]).astype(jnp.bfloat16)

    @pl.when(i % tiles_per_seq == 0)
    def _():
        carry_scr[n] = jnp.zeros(carry_scr.shape[1:], jnp.float32)

    u = u_scr[...]
    b_gate = jnp.dot(u, wb_ref[...], preferred_element_type=jnp.float32)
    c_gate = jnp.dot(u, wc_ref[...], preferred_element_type=jnp.float32)
    hid = jnp.dot(u, wh_ref[...], preferred_element_type=jnp.float32)
    gate = jnp.dot(u, wg_ref[...], preferred_element_type=jnp.float32)
    t = c_gate * hid
    tm = t.shape[0]
    carry = carry_scr[n]
    prev1 = carry[7:8, :]
    prev2 = carry[6:7, :]
    row = lax.broadcasted_iota(jnp.int32, t.shape, 0)
    t1 = jnp.where(row == 0, prev1, pltpu.roll(t, 1, 0))
    t2 = jnp.where(row == 0, prev2, jnp.where(row == 1, prev1, pltpu.roll(t, 2, 0)))
    carry_scr[n] = t[tm - 8:, :]
    cw = cw_ref[...]
    conv = cw[2:3, :] * t + cw[1:2, :] * t1 + cw[0:1, :] * t2
    o_ref[...] = (gate * _sigmoid(gate) * b_gate * conv).astype(jnp.bfloat16)


def _conv_in(h2d, g, w_in, conv_w, s_len, tm, tn):
    n_tok, d = h2d.shape
    width = w_in.shape[1] // 4
    nb = width // tn
    w_spec = lambda part: pl.BlockSpec((d, tn), lambda i, n: (0, part * nb + n))
    return pl.pallas_call(
        functools.partial(_conv_in_kernel, s_len // tm),
        grid=(n_tok // tm, nb),
        in_specs=[
            pl.BlockSpec((tm, d), lambda i, n: (i, 0)),
            pl.BlockSpec((1, d), lambda i, n: (0, 0)),
            w_spec(0), w_spec(1), w_spec(2), w_spec(3),
            pl.BlockSpec((conv_w.shape[0], tn), lambda i, n: (0, n)),
        ],
        out_specs=pl.BlockSpec((tm, tn), lambda i, n: (i, n)),
        out_shape=jax.ShapeDtypeStruct((n_tok, width), jnp.bfloat16),
        scratch_shapes=[
            pltpu.VMEM((tm, d), jnp.bfloat16),
            pltpu.VMEM((nb, 8, tn), jnp.float32),
        ],
        compiler_params=_params(("arbitrary", "arbitrary")),
        name="conv_in_proj",
    )(h2d, g, w_in, w_in, w_in, w_in, conv_w)


def kernel(x, p, positions, norm_g, a_w_in, a_q_norm_g, a_kv_norm_g, a_w_q_up, a_w_kv_up,
           a_qn_g, a_kn_g, a_w_out, b_w_in, b_conv_w, b_w_out, ple_w_proj, ple_norm_g,
           ple_w_gate, ple_b_gate):
    bsz, s_len, d = x.shape
    depth = p.shape[0]
    ple_dim = p.shape[-1]
    assert depth == 2 and a_w_in.shape[0] == 1 and b_w_in.shape[0] == 1
    q_lora = a_q_norm_g.shape[-1]
    kv_lora = a_kv_norm_g.shape[-1]
    qk = a_qn_g.shape[-1]
    att_width = a_w_out.shape[1]
    rope = a_w_in.shape[-1] - q_lora - kv_lora - att_width
    nope = qk - rope
    n_heads = a_w_q_up.shape[-1] // qk
    v_dim = a_w_kv_up.shape[-1] // n_heads - nope
    assert n_heads * v_dim == att_width
    n_tok = bsz * s_len
    bf = jnp.bfloat16

    tm, tm_proj, bq, bk, tm_conv, tn_conv = _tile_plan(s_len, b_w_out.shape[1])
    assert s_len % tm == 0 and s_len % tm_proj == 0 and s_len % tm_conv == 0

    w_in0 = a_w_in[0]
    n_lat = q_lora + kv_lora
    w_lat = w_in0[:, :n_lat + rope].astype(bf)
    w_gate0 = w_in0[:, n_lat + rope:].astype(bf)
    wq_t = a_w_q_up[0].T.astype(bf)
    w_kv = a_w_kv_up[0].reshape(kv_lora, n_heads, nope + v_dim)
    wk = w_kv[:, :, :nope].reshape(kv_lora, n_heads * nope).astype(bf)
    wv_t = w_kv[:, :, nope:].reshape(kv_lora, n_heads * v_dim).T.astype(bf)
    gq_b = jnp.broadcast_to((a_qn_g[0] * (qk ** -0.5 * LOG2_E))[:, None], (qk, tm_proj))
    gk = a_kn_g[0]
    gk_nope = gk[None, :nope]
    gk_rope_b = jnp.broadcast_to(gk[nope:, None], (rope, tm_proj))

    h = x.reshape(n_tok, d)
    p2 = p.reshape(depth, n_tok, ple_dim)

    tables = _rope_tables(positions, rope)
    dims = (n_heads, q_lora, kv_lora, nope, rope, v_dim)
    sgate, q_t, k, v_t = _mla_proj(h, norm_g[0][None, :], w_lat, w_gate0, tables,
                                   a_q_norm_g[0][None, :], a_kv_norm_g[0][None, :],
                                   wq_t, wk, wv_t, gq_b, gk_nope, gk_rope_b,
                                   dims, bsz, s_len, tm_proj)
    act = _attention(q_t, k, v_t, sgate.reshape(bsz, s_len, att_width), bq, bk, 2)
    h = _out_ple(act.reshape(n_tok, att_width), h, p2, 0, a_w_out[0].astype(bf),
                 ple_w_proj[0].astype(bf), ple_norm_g[0][None, :], ple_w_gate[0].astype(bf),
                 ple_b_gate[0][None, :], tm)

    act = _conv_in(h, norm_g[1][None, :], b_w_in[0].astype(bf), b_conv_w[0], s_len,
                   tm_conv, tn_conv)
    h = _out_ple(act, h, p2, 1, b_w_out[0].astype(bf), ple_w_proj[1].astype(bf),
                 ple_norm_g[1][None, :], ple_w_gate[1].astype(bf), ple_b_gate[1][None, :], tm)
    return h.reshape(bsz, s_len, d)
```

```python
import functools

import jax
import jax.numpy as jnp
from jax import lax
from jax.experimental import pallas as pl
from jax.experimental.pallas import tpu as pltpu

EPS = 1e-6
ROPE_THETA = 10000.0
MASK_VALUE = -1e30
LOG2_E = 1.4426950408889634
ONES_ROWS = 16
SCORE_LEAD = 1
VMEM_LIMIT_BYTES = 56 * 1024 * 1024
ATTN_VMEM_LIMIT_BYTES = 60 * 1024 * 1024

_NT = (((1,), (1,)), ((), ()))


def _tile_plan(s_len, conv_width):
    bq = min(2048, s_len)
    return (min(512, s_len), min(256, s_len), bq, min(2, s_len // bq), min(512, s_len),
            min(1024, s_len), min(512, conv_width))


def _params(semantics, vmem_limit=VMEM_LIMIT_BYTES):
    return pltpu.CompilerParams(dimension_semantics=semantics, vmem_limit_bytes=vmem_limit)


def _resident(shape):
    nd = len(shape)
    return pl.BlockSpec(shape, lambda *_: (0,) * nd, pipeline_mode=pl.Buffered(1))


def _rms_scale(x):
    return lax.rsqrt(jnp.mean(x * x, axis=-1, keepdims=True) + EPS)


def _sigmoid(x):
    return 1.0 / (1.0 + jnp.exp(-x))


def _rope_kernel(pos_ref, invf_ref, cos_t_ref, sin_t_ref):
    ang_t = invf_ref[...] * pos_ref[0].astype(jnp.float32)
    cos_t_ref[0] = jnp.cos(ang_t)
    sin_t_ref[0] = jnp.sin(ang_t)


def _rope_tables(positions, rope_dim):
    bsz, s_len = positions.shape
    half = rope_dim // 2
    ts = min(s_len, 2048)
    inv_freq = ROPE_THETA ** (-jnp.arange(0, rope_dim, 2, dtype=jnp.float32) / rope_dim)
    table = pl.BlockSpec((1, half, ts), lambda b, i: (b, 0, i))
    return pl.pallas_call(
        _rope_kernel,
        grid=(bsz, s_len // ts),
        in_specs=[
            pl.BlockSpec((1, 1, ts), lambda b, i: (b, 0, i)),
            pl.BlockSpec((half, 1), lambda b, i: (0, 0)),
        ],
        out_specs=[table, table],
        out_shape=[jax.ShapeDtypeStruct((bsz, half, s_len), jnp.float32)] * 2,
        compiler_params=_params(("parallel", "parallel")),
        name="rope_tables",
    )(positions[:, None, :], inv_freq[:, None])


def _mla_proj_kernel(dims, h_ref, g_ref, w_lat_ref, w_gate_ref, cos_t_ref, sin_t_ref,
                     gq_lat_ref, gkv_lat_ref, wq_t_ref, wk_ref, wv_t_ref,
                     gq_ref, gk_nope_ref, gk_rope_ref,
                     sgate_out, q_out, k_out, v_out, q_scr, k_scr):
    n_heads, q_lora, kv_lora, nope, rope, v_dim = dims
    qk = nope + rope
    half = rope // 2
    h = h_ref[...]
    u = (h * _rms_scale(h) * g_ref[...]).astype(jnp.bfloat16)
    lat = jnp.dot(u, w_lat_ref[...], preferred_element_type=jnp.float32)
    c_q = lat[:, :q_lora]
    c_kv = lat[:, q_lora:q_lora + kv_lora]
    k_r = lat[:, q_lora + kv_lora:q_lora + kv_lora + rope]
    cq_n = (c_q * _rms_scale(c_q) * gq_lat_ref[...]).astype(jnp.bfloat16)
    ckv_n = (c_kv * _rms_scale(c_kv) * gkv_lat_ref[...]).astype(jnp.bfloat16)

    k_scr[...] = jnp.dot(ckv_n, wk_ref[...], preferred_element_type=jnp.float32)
    q_scr[...] = lax.dot_general(wq_t_ref[...], cq_n, _NT, preferred_element_type=jnp.float32)
    v_t = lax.dot_general(wv_t_ref[...], ckv_n, _NT, preferred_element_type=jnp.float32)
    gate = jnp.dot(u, w_gate_ref[...], preferred_element_type=jnp.float32)
    cos_t = cos_t_ref[0]
    sin_t = sin_t_ref[0]

    ss_rope = jnp.sum(k_r * k_r, axis=-1, keepdims=True)
    kr_t = k_r.T * gk_rope_ref[...]
    kr1 = kr_t[:half]
    kr2 = kr_t[half:]
    rot = jnp.concatenate([kr1 * cos_t - kr2 * sin_t, kr2 * cos_t + kr1 * sin_t], axis=0).T
    gk_nope = gk_nope_ref[...]
    for hd in range(n_heads):
        blk = k_scr[:, hd * nope:(hd + 1) * nope]
        ss = jnp.sum(blk * blk, axis=-1, keepdims=True) + ss_rope
        rinv = lax.rsqrt(ss * (1.0 / qk) + EPS)
        k_out[0, hd, :, :nope] = (blk * rinv * gk_nope).astype(jnp.bfloat16)
        k_out[0, hd, :, nope:] = (rot * rinv).astype(jnp.bfloat16)

    gq = gq_ref[...]
    for hd in range(n_heads):
        blk = q_scr[hd * qk:(hd + 1) * qk, :]
        rinv = lax.rsqrt(jnp.mean(blk * blk, axis=0, keepdims=True) + EPS)
        qn = blk * rinv * gq
        r1 = qn[nope:nope + half]
        r2 = qn[nope + half:]
        q_out[0, hd, :nope, :] = qn[:nope].astype(jnp.bfloat16)
        q_out[0, hd, nope:nope + half, :] = (r1 * cos_t - r2 * sin_t).astype(jnp.bfloat16)
        q_out[0, hd, nope + half:, :] = (r2 * cos_t + r1 * sin_t).astype(jnp.bfloat16)

    v_out[0, :, :v_dim, :] = v_t.astype(jnp.bfloat16).reshape(n_heads, v_dim, v_t.shape[-1])
    v_out[0, :, v_dim:, :] = jnp.ones((n_heads, ONES_ROWS, v_t.shape[-1]), jnp.bfloat16)
    sgate_out[...] = (gate * _sigmoid(gate)).astype(jnp.bfloat16)


def _mla_proj(h2d, g, w_lat, w_gate, tables, gq_lat, gkv_lat, wq_t, wk, wv_t,
              gq_b, gk_nope, gk_rope_b, dims, bsz, s_len, tm):
    n_heads, q_lora, kv_lora, nope, rope, v_dim = dims
    n_tok, d = h2d.shape
    n_gate = w_gate.shape[1]
    qk = nope + rope
    half = rope // 2
    cos_t, sin_t = tables
    spb = s_len // tm
    table = pl.BlockSpec((1, half, tm), lambda i: (i // spb, 0, i % spb))
    weights = (g, w_lat, w_gate)
    consts = (gq_lat, gkv_lat, wq_t, wk, wv_t, gq_b, gk_nope, gk_rope_b)
    return pl.pallas_call(
        functools.partial(_mla_proj_kernel, dims),
        grid=(bsz * spb,),
        in_specs=([pl.BlockSpec((tm, d), lambda i: (i, 0))]
                  + [_resident(a.shape) for a in weights]
                  + [table, table]
                  + [_resident(a.shape) for a in consts]),
        out_specs=[
            pl.BlockSpec((tm, n_gate), lambda i: (i, 0)),
            pl.BlockSpec((1, n_heads, qk, tm), lambda i: (i // spb, 0, 0, i % spb)),
            pl.BlockSpec((1, n_heads, tm, qk), lambda i: (i // spb, 0, i % spb, 0)),
            pl.BlockSpec((1, n_heads, v_dim + ONES_ROWS, tm), lambda i: (i // spb, 0, 0, i % spb)),
        ],
        out_shape=[
            jax.ShapeDtypeStruct((n_tok, n_gate), jnp.bfloat16),
            jax.ShapeDtypeStruct((bsz, n_heads, qk, s_len), jnp.bfloat16),
            jax.ShapeDtypeStruct((bsz, n_heads, s_len, qk), jnp.bfloat16),
            jax.ShapeDtypeStruct((bsz, n_heads, v_dim + ONES_ROWS, s_len), jnp.bfloat16),
        ],
        scratch_shapes=[
            pltpu.VMEM((n_heads * qk, tm), jnp.float32),
            pltpu.VMEM((tm, n_heads * nope), jnp.float32),
        ],
        compiler_params=_params(("parallel",)),
        name="mla_proj",
    )(h2d, *weights, cos_t, sin_t, *consts)


def _attn_kernel(bq, bk, hps, v_dim, nsub, q_ref, k_ref, v_ref, sg_ref, o_ref, *scratch):
    for sub in range(nsub):
        _attn_tile(bq, bk, hps, v_dim, pl.program_id(2) * nsub + sub, sub * bq, sub > 0,
                   q_ref, k_ref, v_ref, sg_ref, o_ref, *scratch)


def _attn_tile(bq, bk, hps, v_dim, qi, qoff, qi_positive, q_ref, k_ref, v_ref, sg_ref, o_ref,
               s_ref, m_ref, alpha_ref, acc_ref):
    ncol = bq // bk
    first_diag = qi * ncol
    nslots = s_ref.shape[0]
    m_ref[...] = jnp.full(m_ref.shape, MASK_VALUE, jnp.float32)
    acc_ref[...] = jnp.zeros(acc_ref.shape, jnp.float32)

    def chains(t):
        return [(hh, c) for hh in range(hps) for c in range(ncol) if t is None or c >= t]

    def score_matmul(j, hh, c):
        start = pl.multiple_of(j * bk, bk)
        q_strip = q_ref[0, hh, :, qoff + c * bk:qoff + (c + 1) * bk]
        return jnp.dot(k_ref[0, hh, pl.ds(start, bk), :], q_strip,
                       preferred_element_type=jnp.float32)

    def finish_chain(j, hh, c, slot):
        start = pl.multiple_of(j * bk, bk)
        i = hh * ncol + c
        p = jnp.exp2(s_ref[slot, i] - m_ref[i]).astype(jnp.bfloat16)
        v_blk = v_ref[0, hh, :, pl.ds(start, bk)]
        pv = jnp.dot(v_blk, p, preferred_element_type=jnp.float32)
        acc_ref[i] = alpha_ref[i] * acc_ref[i] + pv

    def score_matmuls(j, t):
        return {(hh, c): score_matmul(j, hh, c) for hh, c in chains(t)}

    def finish_block(j, t, slot):
        for hh, c in chains(t):
            finish_chain(j, hh, c, slot)

    def stage_block(scores, mask_of, slot):
        for (hh, c), s_t in scores.items():
            i = hh * ncol + c
            mask = mask_of(c)
            if mask is not None:
                s_t = jnp.where(mask, s_t, MASK_VALUE)
            m_prev = m_ref[i]
            m_new = jnp.maximum(m_prev, jnp.max(s_t, axis=0, keepdims=True))
            s_ref[slot, i] = s_t
            alpha_ref[i] = jnp.exp2(m_prev - m_new)
            m_ref[i] = m_new

    row = lax.broadcasted_iota(jnp.int32, (bk, bk), 0)
    col = lax.broadcasted_iota(jnp.int32, (bk, bk), 1)

    def pipelined(j, t, slot):
        todo_s = chains(t)
        todo_f = chains(None if not t else t - 1)
        scores = {}
        n_s = n_f = 0
        while n_s < len(todo_s) or n_f < len(todo_f):
            if n_s < len(todo_s) and (n_s < n_f + SCORE_LEAD or n_f == len(todo_f)):
                scores[todo_s[n_s]] = score_matmul(j, *todo_s[n_s])
                n_s += 1
            else:
                finish_chain(j - 1, *todo_f[n_f], (slot - 1) % nslots)
                n_f += 1
        stage_block(scores, lambda c: (row <= col) if c == t else None, slot)

    stage_block(score_matmuls(0, None),
                lambda c: (row <= qi * bq + col) if c == 0 else None, 0)

    if nslots == 1:
        def body(j, carry):
            pipelined(j, None, 0)
            return carry

        lax.fori_loop(1, first_diag, body, 0)
    else:
        def body(u, carry):
            pipelined(2 * u - 1, None, 1)
            pipelined(2 * u, None, 0)
            return carry

        lax.fori_loop(1, first_diag // 2, body, 0)

    def last_below_and_first_diagonal():
        if nslots == 2:
            pipelined(first_diag - 1, None, 1)
        pipelined(first_diag, 0, 0)

    if qi_positive:
        last_below_and_first_diagonal()
    else:
        pl.when(qi >= 1)(last_below_and_first_diagonal)

    for t in range(1, ncol):
        pipelined(first_diag + t, t, t % nslots)

    finish_block(first_diag + ncol - 1, ncol - 1, (ncol - 1) % nslots)

    for hh in range(hps):
        for c in range(ncol):
            rows = slice(qoff + c * bk, qoff + (c + 1) * bk)
            cols = slice(hh * v_dim, (hh + 1) * v_dim)
            acc = acc_ref[hh * ncol + c]
            o_t = acc[:v_dim] * (1.0 / acc[v_dim:v_dim + 1])
            o_ref[0, rows, cols] = (o_t.T * sg_ref[0, rows, cols].astype(jnp.float32)
                                    ).astype(jnp.bfloat16)


def _attention(q_t, k, v_t, sgate, bq, bk, hps, nsub):
    bsz, n_heads, qk, s_len = q_t.shape
    v_rows = v_t.shape[2]
    v_dim = v_rows - ONES_ROWS
    span = bq * nsub
    assert bq % bk == 0 and s_len % span == 0 and n_heads % hps == 0
    n_chains = hps * (bq // bk)
    nslots = 2 if (bq // bk) % 2 == 0 else 1
    return pl.pallas_call(
        functools.partial(_attn_kernel, bq, bk, hps, v_dim, nsub),
        grid=(bsz, n_heads // hps, s_len // span),
        in_specs=[
            pl.BlockSpec((1, hps, qk, span), lambda b, h, i: (b, h, 0, i)),
            pl.BlockSpec((1, hps, s_len, qk), lambda b, h, i: (b, h, 0, 0)),
            pl.BlockSpec((1, hps, v_rows, s_len), lambda b, h, i: (b, h, 0, 0)),
            pl.BlockSpec((1, span, hps * v_dim), lambda b, h, i: (b, i, h)),
        ],
        out_specs=pl.BlockSpec((1, span, hps * v_dim), lambda b, h, i: (b, i, h)),
        out_shape=jax.ShapeDtypeStruct((bsz, s_len, n_heads * v_dim), jnp.bfloat16),
        scratch_shapes=[
            pltpu.VMEM((nslots, n_chains, bk, bk), jnp.float32),
            pltpu.VMEM((n_chains, 1, bk), jnp.float32),
            pltpu.VMEM((n_chains, 1, bk), jnp.float32),
            pltpu.VMEM((n_chains, v_rows, bk), jnp.float32),
        ],
        compiler_params=_params(("parallel", "parallel", "arbitrary"), ATTN_VMEM_LIMIT_BYTES),
        name="mla_attention",
    )(q_t, k, v_t, sgate)


def _out_ple_kernel(act_ref, h_ref, p_ref, w_out_ref, w_proj_ref, g_ref, w_gate_ref, b_ref, o_ref):
    y = jnp.dot(act_ref[...], w_out_ref[...], preferred_element_type=jnp.float32)
    h1 = h_ref[...] + y
    e = jnp.dot(p_ref[0].astype(jnp.bfloat16), w_proj_ref[...], preferred_element_type=jnp.float32)
    e = e * _rms_scale(e) * g_ref[...]
    gate = jnp.dot(h1.astype(jnp.bfloat16), w_gate_ref[...], preferred_element_type=jnp.float32)
    o_ref[...] = h1 + _sigmoid(gate + b_ref[...]) * e


def _out_ple(act, h2d, p_all, layer, w_out, w_proj, g, w_gate, b, tm):
    n_tok, d = h2d.shape
    width = act.shape[1]
    ple = p_all.shape[-1]
    return pl.pallas_call(
        _out_ple_kernel,
        grid=(n_tok // tm,),
        in_specs=[
            pl.BlockSpec((tm, width), lambda i: (i, 0)),
            pl.BlockSpec((tm, d), lambda i: (i, 0)),
            pl.BlockSpec((1, tm, ple), lambda i: (layer, i, 0)),
            _resident(w_out.shape),
            _resident(w_proj.shape),
            _resident(g.shape),
            _resident(w_gate.shape),
            _resident(b.shape),
        ],
        out_specs=pl.BlockSpec((tm, d), lambda i: (i, 0)),
        out_shape=jax.ShapeDtypeStruct((n_tok, d), jnp.float32),
        compiler_params=_params(("parallel",)),
        name="out_proj_ple",
    )(act, h2d, p_all, w_out, w_proj, g, w_gate, b)


def _conv_in_kernel(tiles_per_seq, h_ref, g_ref, wb_ref, wc_ref, wh_ref, wg_ref, cw_ref,
                    o_ref, u_scr, carry_scr):
    i = pl.program_id(0)
    n = pl.program_id(1)

    @pl.when(n == 0)
    def _():
        h = h_ref[...]
        u_scr[...] = (h * _rms_scale(h) * g_ref[...]).astype(jnp.bfloat16)

    @pl.when(i % tiles_per_seq == 0)
    def _():
        carry_scr[n] = jnp.zeros(carry_scr.shape[1:], jnp.float32)

    u = u_scr[...]
    b_gate = jnp.dot(u, wb_ref[...], preferred_element_type=jnp.float32)
    c_gate = jnp.dot(u, wc_ref[...], preferred_element_type=jnp.float32)
    hid = jnp.dot(u, wh_ref[...], preferred_element_type=jnp.float32)
    gate = jnp.dot(u, wg_ref[...], preferred_element_type=jnp.float32)
    t = c_gate * hid
    tm = t.shape[0]
    carry = carry_scr[n]
    prev1 = carry[7:8, :]
    prev2 = carry[6:7, :]
    row = lax.broadcasted_iota(jnp.int32, t.shape, 0)
    t1 = jnp.where(row == 0, prev1, pltpu.roll(t, 1, 0))
    t2 = jnp.where(row == 0, prev2, jnp.where(row == 1, prev1, pltpu.roll(t, 2, 0)))
    carry_scr[n] = t[tm - 8:, :]
    cw = cw_ref[...]
    conv = cw[2:3, :] * t + cw[1:2, :] * t1 + cw[0:1, :] * t2
    o_ref[...] = (gate * _sigmoid(gate) * b_gate * conv).astype(jnp.bfloat16)


def _conv_in(h2d, g, w_in, conv_w, s_len, tm, tn):
    n_tok, d = h2d.shape
    width = w_in.shape[1] // 4
    nb = width // tn
    w_spec = lambda part: pl.BlockSpec((d, tn), lambda i, n: (0, part * nb + n))
    return pl.pallas_call(
        functools.partial(_conv_in_kernel, s_len // tm),
        grid=(n_tok // tm, nb),
        in_specs=[
            pl.BlockSpec((tm, d), lambda i, n: (i, 0)),
            pl.BlockSpec((1, d), lambda i, n: (0, 0)),
            w_spec(0), w_spec(1), w_spec(2), w_spec(3),
            pl.BlockSpec((conv_w.shape[0], tn), lambda i, n: (0, n)),
        ],
        out_specs=pl.BlockSpec((tm, tn), lambda i, n: (i, n)),
        out_shape=jax.ShapeDtypeStruct((n_tok, width), jnp.bfloat16),
        scratch_shapes=[
            pltpu.VMEM((tm, d), jnp.bfloat16),
            pltpu.VMEM((nb, 8, tn), jnp.float32),
        ],
        compiler_params=_params(("arbitrary", "arbitrary")),
        name="conv_in_proj",
    )(h2d, g, w_in, w_in, w_in, w_in, conv_w)


def kernel(x, p, positions, norm_g, a_w_in, a_q_norm_g, a_kv_norm_g, a_w_q_up, a_w_kv_up,
           a_qn_g, a_kn_g, a_w_out, b_w_in, b_conv_w, b_w_out, ple_w_proj, ple_norm_g,
           ple_w_gate, ple_b_gate):
    bsz, s_len, d = x.shape
    depth = p.shape[0]
    ple_dim = p.shape[-1]
    assert depth == 2 and a_w_in.shape[0] == 1 and b_w_in.shape[0] == 1
    q_lora = a_q_norm_g.shape[-1]
    kv_lora = a_kv_norm_g.shape[-1]
    qk = a_qn_g.shape[-1]
    att_width = a_w_out.shape[1]
    rope = a_w_in.shape[-1] - q_lora - kv_lora - att_width
    nope = qk - rope
    n_heads = a_w_q_up.shape[-1] // qk
    v_dim = a_w_kv_up.shape[-1] // n_heads - nope
    assert n_heads * v_dim == att_width
    n_tok = bsz * s_len
    bf = jnp.bfloat16

    tm, tm_proj, bq, nsub, bk, tm_conv, tn_conv = _tile_plan(s_len, b_w_out.shape[1])
    assert s_len % tm == 0 and s_len % tm_proj == 0 and s_len % tm_conv == 0

    w_in0 = a_w_in[0]
    n_lat = q_lora + kv_lora
    w_lat = w_in0[:, :n_lat + rope].astype(bf)
    w_gate0 = w_in0[:, n_lat + rope:].astype(bf)
    wq_t = a_w_q_up[0].T.astype(bf)
    w_kv = a_w_kv_up[0].reshape(kv_lora, n_heads, nope + v_dim)
    wk = w_kv[:, :, :nope].reshape(kv_lora, n_heads * nope).astype(bf)
    wv_t = w_kv[:, :, nope:].reshape(kv_lora, n_heads * v_dim).T.astype(bf)
    gq_b = jnp.broadcast_to((a_qn_g[0] * (qk ** -0.5 * LOG2_E))[:, None], (qk, tm_proj))
    gk = a_kn_g[0]
    gk_nope = gk[None, :nope]
    gk_rope_b = jnp.broadcast_to(gk[nope:, None], (rope, tm_proj))

    h = x.reshape(n_tok, d)
    p2 = p.reshape(depth, n_tok, ple_dim)

    tables = _rope_tables(positions, rope)
    dims = (n_heads, q_lora, kv_lora, nope, rope, v_dim)
    sgate, q_t, k, v_t = _mla_proj(h, norm_g[0][None, :], w_lat, w_gate0, tables,
                                   a_q_norm_g[0][None, :], a_kv_norm_g[0][None, :],
                                   wq_t, wk, wv_t, gq_b, gk_nope, gk_rope_b,
                                   dims, bsz, s_len, tm_proj)
    act = _attention(q_t, k, v_t, sgate.reshape(bsz, s_len, att_width), bq, bk, 2, nsub)
    h = _out_ple(act.reshape(n_tok, att_width), h, p2, 0, a_w_out[0].astype(bf),
                 ple_w_proj[0].astype(bf), ple_norm_g[0][None, :], ple_w_gate[0].astype(bf),
                 ple_b_gate[0][None, :], tm)

    act = _conv_in(h, norm_g[1][None, :], b_w_in[0].astype(bf), b_conv_w[0], s_len,
                   tm_conv, tn_conv)
    h = _out_ple(act, h, p2, 1, b_w_out[0].astype(bf), ple_w_proj[1].astype(bf),
                 ple_norm_g[1][None, :], ple_w_gate[1].astype(bf), ple_b_gate[1][None, :], tm)
    return h.reshape(bsz, s_len, d)
```

```python
import functools

import jax
import jax.numpy as jnp
from jax import lax
from jax.experimental import pallas as pl
from jax.experimental.pallas import tpu as pltpu

EPS = 1e-6
ROPE_THETA = 10000.0
MASK_VALUE = -1e30
LOG2_E = 1.4426950408889634
ONES_ROWS = 16
SCORE_LEAD = 1
VMEM_LIMIT_BYTES = 56 * 1024 * 1024
ATTN_VMEM_LIMIT_BYTES = 60 * 1024 * 1024

_NT = (((1,), (1,)), ((), ()))


def _tile_plan(s_len, conv_width):
    bq = min(2048, s_len)
    return (min(512, s_len), min(256, s_len), bq, min(2, s_len // bq), min(512, s_len),
            min(1024, s_len), min(512, conv_width))


def _params(semantics, vmem_limit=VMEM_LIMIT_BYTES):
    return pltpu.CompilerParams(dimension_semantics=semantics, vmem_limit_bytes=vmem_limit)


def _resident(shape):
    nd = len(shape)
    return pl.BlockSpec(shape, lambda *_: (0,) * nd, pipeline_mode=pl.Buffered(1))


def _rms_scale(x):
    return lax.rsqrt(jnp.mean(x * x, axis=-1, keepdims=True) + EPS)


def _sigmoid(x):
    return 1.0 / (1.0 + jnp.exp(-x))


def _rope_kernel(pos_ref, invf_ref, cos_t_ref, sin_t_ref):
    ang_t = invf_ref[...] * pos_ref[0].astype(jnp.float32)
    cos_t_ref[0] = jnp.cos(ang_t)
    sin_t_ref[0] = jnp.sin(ang_t)


def _rope_tables(positions, rope_dim):
    bsz, s_len = positions.shape
    half = rope_dim // 2
    ts = min(s_len, 2048)
    inv_freq = ROPE_THETA ** (-jnp.arange(0, rope_dim, 2, dtype=jnp.float32) / rope_dim)
    table = pl.BlockSpec((1, half, ts), lambda b, i: (b, 0, i))
    return pl.pallas_call(
        _rope_kernel,
        grid=(bsz, s_len // ts),
        in_specs=[
            pl.BlockSpec((1, 1, ts), lambda b, i: (b, 0, i)),
            pl.BlockSpec((half, 1), lambda b, i: (0, 0)),
        ],
        out_specs=[table, table],
        out_shape=[jax.ShapeDtypeStruct((bsz, half, s_len), jnp.float32)] * 2,
        compiler_params=_params(("parallel", "parallel")),
        name="rope_tables",
    )(positions[:, None, :], inv_freq[:, None])


def _mla_proj_kernel(dims, h_ref, g_ref, w_lat_ref, w_gate_ref, cos_t_ref, sin_t_ref,
                     gq_lat_ref, gkv_lat_ref, wq_t_ref, wk_ref, wv_t_ref,
                     gq_ref, gk_nope_ref, gk_rope_ref,
                     sgate_out, q_out, k_out, v_out, q_scr, k_scr):
    n_heads, q_lora, kv_lora, nope, rope, v_dim = dims
    qk = nope + rope
    half = rope // 2
    h = h_ref[...]
    u = (h * _rms_scale(h) * g_ref[...]).astype(jnp.bfloat16)
    lat = jnp.dot(u, w_lat_ref[...], preferred_element_type=jnp.float32)
    c_q = lat[:, :q_lora]
    c_kv = lat[:, q_lora:q_lora + kv_lora]
    k_r = lat[:, q_lora + kv_lora:q_lora + kv_lora + rope]
    cq_n = (c_q * _rms_scale(c_q) * gq_lat_ref[...]).astype(jnp.bfloat16)
    ckv_n = (c_kv * _rms_scale(c_kv) * gkv_lat_ref[...]).astype(jnp.bfloat16)

    k_scr[...] = jnp.dot(ckv_n, wk_ref[...], preferred_element_type=jnp.float32)
    q_scr[...] = lax.dot_general(wq_t_ref[...], cq_n, _NT, preferred_element_type=jnp.float32)
    v_t = lax.dot_general(wv_t_ref[...], ckv_n, _NT, preferred_element_type=jnp.float32)
    gate = jnp.dot(u, w_gate_ref[...], preferred_element_type=jnp.float32)
    cos_t = cos_t_ref[0]
    sin_t = sin_t_ref[0]

    ss_rope = jnp.sum(k_r * k_r, axis=-1, keepdims=True)
    kr_t = k_r.T * gk_rope_ref[...]
    kr1 = kr_t[:half]
    kr2 = kr_t[half:]
    rot = jnp.concatenate([kr1 * cos_t - kr2 * sin_t, kr2 * cos_t + kr1 * sin_t], axis=0).T
    gk_nope = gk_nope_ref[...]
    for hd in range(n_heads):
        blk = k_scr[:, hd * nope:(hd + 1) * nope]
        ss = jnp.sum(blk * blk, axis=-1, keepdims=True) + ss_rope
        rinv = lax.rsqrt(ss * (1.0 / qk) + EPS)
        k_out[0, hd, :, :nope] = (blk * rinv * gk_nope).astype(jnp.bfloat16)
        k_out[0, hd, :, nope:] = (rot * rinv).astype(jnp.bfloat16)

    gq = gq_ref[...]
    for hd in range(n_heads):
        blk = q_scr[hd * qk:(hd + 1) * qk, :]
        rinv = lax.rsqrt(jnp.mean(blk * blk, axis=0, keepdims=True) + EPS)
        qn = blk * rinv * gq
        r1 = qn[nope:nope + half]
        r2 = qn[nope + half:]
        q_out[0, hd, :nope, :] = qn[:nope].astype(jnp.bfloat16)
        q_out[0, hd, nope:nope + half, :] = (r1 * cos_t - r2 * sin_t).astype(jnp.bfloat16)
        q_out[0, hd, nope + half:, :] = (r2 * cos_t + r1 * sin_t).astype(jnp.bfloat16)

    v_out[0, :, :v_dim, :] = v_t.astype(jnp.bfloat16).reshape(n_heads, v_dim, v_t.shape[-1])
    v_out[0, :, v_dim:, :] = jnp.ones((n_heads, ONES_ROWS, v_t.shape[-1]), jnp.bfloat16)
    sgate_out[...] = (gate * _sigmoid(gate)).astype(jnp.bfloat16)


def _mla_proj(h2d, g, w_lat, w_gate, tables, gq_lat, gkv_lat, wq_t, wk, wv_t,
              gq_b, gk_nope, gk_rope_b, dims, bsz, s_len, tm):
    n_heads, q_lora, kv_lora, nope, rope, v_dim = dims
    n_tok, d = h2d.shape
    n_gate = w_gate.shape[1]
    qk = nope + rope
    half = rope // 2
    cos_t, sin_t = tables
    spb = s_len // tm
    table = pl.BlockSpec((1, half, tm), lambda i: (i // spb, 0, i % spb))
    weights = (g, w_lat, w_gate)
    consts = (gq_lat, gkv_lat, wq_t, wk, wv_t, gq_b, gk_nope, gk_rope_b)
    return pl.pallas_call(
        functools.partial(_mla_proj_kernel, dims),
        grid=(bsz * spb,),
        in_specs=([pl.BlockSpec((tm, d), lambda i: (i, 0))]
                  + [_resident(a.shape) for a in weights]
                  + [table, table]
                  + [_resident(a.shape) for a in consts]),
        out_specs=[
            pl.BlockSpec((tm, n_gate), lambda i: (i, 0)),
            pl.BlockSpec((1, n_heads, qk, tm), lambda i: (i // spb, 0, 0, i % spb)),
            pl.BlockSpec((1, n_heads, tm, qk), lambda i: (i // spb, 0, i % spb, 0)),
            pl.BlockSpec((1, n_heads, v_dim + ONES_ROWS, tm), lambda i: (i // spb, 0, 0, i % spb)),
        ],
        out_shape=[
            jax.ShapeDtypeStruct((n_tok, n_gate), jnp.bfloat16),
            jax.ShapeDtypeStruct((bsz, n_heads, qk, s_len), jnp.bfloat16),
            jax.ShapeDtypeStruct((bsz, n_heads, s_len, qk), jnp.bfloat16),
            jax.ShapeDtypeStruct((bsz, n_heads, v_dim + ONES_ROWS, s_len), jnp.bfloat16),
        ],
        scratch_shapes=[
            pltpu.VMEM((n_heads * qk, tm), jnp.float32),
            pltpu.VMEM((tm, n_heads * nope), jnp.float32),
        ],
        compiler_params=_params(("parallel",)),
        name="mla_proj",
    )(h2d, *weights, cos_t, sin_t, *consts)


def _attn_kernel(bq, bk, hps, v_dim, nsub, q_ref, k_ref, v_ref, sg_ref, o_ref, *scratch):
    for sub in range(nsub):
        _attn_tile(bq, bk, hps, v_dim, pl.program_id(2) * nsub + sub, sub * bq, sub > 0,
                   q_ref, k_ref, v_ref, sg_ref, o_ref, *scratch)


def _attn_tile(bq, bk, hps, v_dim, qi, qoff, qi_positive, q_ref, k_ref, v_ref, sg_ref, o_ref,
               s_ref, m_ref, alpha_ref, acc_ref):
    ncol = bq // bk
    first_diag = qi * ncol
    nslots = s_ref.shape[0]
    m_ref[...] = jnp.full(m_ref.shape, MASK_VALUE, jnp.float32)
    acc_ref[...] = jnp.zeros(acc_ref.shape, jnp.float32)

    def chains(t):
        return [(hh, c) for c in range(ncol) for hh in range(hps) if t is None or c >= t]

    def score_matmul(j, hh, c):
        start = pl.multiple_of(j * bk, bk)
        q_strip = q_ref[0, hh, :, qoff + c * bk:qoff + (c + 1) * bk]
        return jnp.dot(k_ref[0, hh, pl.ds(start, bk), :], q_strip,
                       preferred_element_type=jnp.float32)

    def finish_chain(j, hh, c, slot):
        start = pl.multiple_of(j * bk, bk)
        i = hh * ncol + c
        p = jnp.exp2(s_ref[slot, i] - m_ref[i]).astype(jnp.bfloat16)
        v_blk = v_ref[0, hh, :, pl.ds(start, bk)]
        pv = jnp.dot(v_blk, p, preferred_element_type=jnp.float32)
        acc_ref[i] = alpha_ref[i] * acc_ref[i] + pv

    def score_matmuls(j, t):
        return {(hh, c): score_matmul(j, hh, c) for hh, c in chains(t)}

    def finish_block(j, t, slot):
        for hh, c in chains(t):
            finish_chain(j, hh, c, slot)

    def stage_block(scores, mask_of, slot):
        for (hh, c), s_t in scores.items():
            i = hh * ncol + c
            mask = mask_of(c)
            if mask is not None:
                s_t = jnp.where(mask, s_t, MASK_VALUE)
            m_prev = m_ref[i]
            m_new = jnp.maximum(m_prev, jnp.max(s_t, axis=0, keepdims=True))
            s_ref[slot, i] = s_t
            alpha_ref[i] = jnp.exp2(m_prev - m_new)
            m_ref[i] = m_new

    row = lax.broadcasted_iota(jnp.int32, (bk, bk), 0)
    col = lax.broadcasted_iota(jnp.int32, (bk, bk), 1)

    def pipelined(j, t, slot):
        todo_s = chains(t)
        todo_f = chains(None if not t else t - 1)
        scores = {}
        n_s = n_f = 0
        while n_s < len(todo_s) or n_f < len(todo_f):
            if n_s < len(todo_s) and (n_s < n_f + SCORE_LEAD or n_f == len(todo_f)):
                scores[todo_s[n_s]] = score_matmul(j, *todo_s[n_s])
                n_s += 1
            else:
                finish_chain(j - 1, *todo_f[n_f], (slot - 1) % nslots)
                n_f += 1
        stage_block(scores, lambda c: (row <= col) if c == t else None, slot)

    stage_block(score_matmuls(0, None),
                lambda c: (row <= qi * bq + col) if c == 0 else None, 0)

    if nslots == 1:
        def body(j, carry):
            pipelined(j, None, 0)
            return carry

        lax.fori_loop(1, first_diag, body, 0)
    else:
        def body(u, carry):
            pipelined(2 * u - 1, None, 1)
            pipelined(2 * u, None, 0)
            return carry

        lax.fori_loop(1, first_diag // 2, body, 0)

    def last_below_and_first_diagonal():
        if nslots == 2:
            pipelined(first_diag - 1, None, 1)
        pipelined(first_diag, 0, 0)

    if qi_positive:
        last_below_and_first_diagonal()
    else:
        pl.when(qi >= 1)(last_below_and_first_diagonal)

    for t in range(1, ncol):
        pipelined(first_diag + t, t, t % nslots)

    finish_block(first_diag + ncol - 1, ncol - 1, (ncol - 1) % nslots)

    for hh in range(hps):
        for c in range(ncol):
            rows = slice(qoff + c * bk, qoff + (c + 1) * bk)
            cols = slice(hh * v_dim, (hh + 1) * v_dim)
            acc = acc_ref[hh * ncol + c]
            o_t = acc[:v_dim] * (1.0 / acc[v_dim:v_dim + 1])
            o_ref[0, rows, cols] = (o_t.T * sg_ref[0, rows, cols].astype(jnp.float32)
                                    ).astype(jnp.bfloat16)


def _attention(q_t, k, v_t, sgate, bq, bk, hps, nsub):
    bsz, n_heads, qk, s_len = q_t.shape
    v_rows = v_t.shape[2]
    v_dim = v_rows - ONES_ROWS
    span = bq * nsub
    assert bq % bk == 0 and s_len % span == 0 and n_heads % hps == 0
    n_chains = hps * (bq // bk)
    nslots = 2 if (bq // bk) % 2 == 0 else 1
    return pl.pallas_call(
        functools.partial(_attn_kernel, bq, bk, hps, v_dim, nsub),
        grid=(bsz, n_heads // hps, s_len // span),
        in_specs=[
            pl.BlockSpec((1, hps, qk, span), lambda b, h, i: (b, h, 0, i)),
            pl.BlockSpec((1, hps, s_len, qk), lambda b, h, i: (b, h, 0, 0)),
            pl.BlockSpec((1, hps, v_rows, s_len), lambda b, h, i: (b, h, 0, 0)),
            pl.BlockSpec((1, span, hps * v_dim), lambda b, h, i: (b, i, h)),
        ],
        out_specs=pl.BlockSpec((1, span, hps * v_dim), lambda b, h, i: (b, i, h)),
        out_shape=jax.ShapeDtypeStruct((bsz, s_len, n_heads * v_dim), jnp.bfloat16),
        scratch_shapes=[
            pltpu.VMEM((nslots, n_chains, bk, bk), jnp.float32),
            pltpu.VMEM((n_chains, 1, bk), jnp.float32),
            pltpu.VMEM((n_chains, 1, bk), jnp.float32),
            pltpu.VMEM((n_chains, v_rows, bk), jnp.float32),
        ],
        compiler_params=_params(("parallel", "parallel", "arbitrary"), ATTN_VMEM_LIMIT_BYTES),
        name="mla_attention",
    )(q_t, k, v_t, sgate)


def _out_ple_kernel(act_ref, h_ref, p_ref, w_out_ref, w_proj_ref, g_ref, w_gate_ref, b_ref, o_ref):
    y = jnp.dot(act_ref[...], w_out_ref[...], preferred_element_type=jnp.float32)
    h1 = h_ref[...] + y
    e = jnp.dot(p_ref[0].astype(jnp.bfloat16), w_proj_ref[...], preferred_element_type=jnp.float32)
    e = e * _rms_scale(e) * g_ref[...]
    gate = jnp.dot(h1.astype(jnp.bfloat16), w_gate_ref[...], preferred_element_type=jnp.float32)
    o_ref[...] = h1 + _sigmoid(gate + b_ref[...]) * e


def _out_ple(act, h2d, p_all, layer, w_out, w_proj, g, w_gate, b, tm):
    n_tok, d = h2d.shape
    width = act.shape[1]
    ple = p_all.shape[-1]
    return pl.pallas_call(
        _out_ple_kernel,
        grid=(n_tok // tm,),
        in_specs=[
            pl.BlockSpec((tm, width), lambda i: (i, 0)),
            pl.BlockSpec((tm, d), lambda i: (i, 0)),
            pl.BlockSpec((1, tm, ple), lambda i: (layer, i, 0)),
            _resident(w_out.shape),
            _resident(w_proj.shape),
            _resident(g.shape),
            _resident(w_gate.shape),
            _resident(b.shape),
        ],
        out_specs=pl.BlockSpec((tm, d), lambda i: (i, 0)),
        out_shape=jax.ShapeDtypeStruct((n_tok, d), jnp.float32),
        compiler_params=_params(("parallel",)),
        name="out_proj_ple",
    )(act, h2d, p_all, w_out, w_proj, g, w_gate, b)


def _conv_in_kernel(tiles_per_seq, h_ref, g_ref, wb_ref, wc_ref, wh_ref, wg_ref, cw_ref,
                    o_ref, u_scr, carry_scr):
    i = pl.program_id(0)
    n = pl.program_id(1)

    @pl.when(n == 0)
    def _():
        h = h_ref[...]
        u_scr[...] = (h * _rms_scale(h) * g_ref[...]).astype(jnp.bfloat16)

    @pl.when(i % tiles_per_seq == 0)
    def _():
        carry_scr[n] = jnp.zeros(carry_scr.shape[1:], jnp.float32)

    u = u_scr[...]
    b_gate = jnp.dot(u, wb_ref[...], preferred_element_type=jnp.float32)
    c_gate = jnp.dot(u, wc_ref[...], preferred_element_type=jnp.float32)
    hid = jnp.dot(u, wh_ref[...], preferred_element_type=jnp.float32)
    gate = jnp.dot(u, wg_ref[...], preferred_element_type=jnp.float32)
    t = c_gate * hid
    tm = t.shape[0]
    carry = carry_scr[n]
    prev1 = carry[7:8, :]
    prev2 = carry[6:7, :]
    row = lax.broadcasted_iota(jnp.int32, t.shape, 0)
    t1 = jnp.where(row == 0, prev1, pltpu.roll(t, 1, 0))
    t2 = jnp.where(row == 0, prev2, jnp.where(row == 1, prev1, pltpu.roll(t, 2, 0)))
    carry_scr[n] = t[tm - 8:, :]
    cw = cw_ref[...]
    conv = cw[2:3, :] * t + cw[1:2, :] * t1 + cw[0:1, :] * t2
    o_ref[...] = (gate * _sigmoid(gate) * b_gate * conv).astype(jnp.bfloat16)


def _conv_in(h2d, g, w_in, conv_w, s_len, tm, tn):
    n_tok, d = h2d.shape
    width = w_in.shape[1] // 4
    nb = width // tn
    w_spec = lambda part: pl.BlockSpec((d, tn), lambda i, n: (0, part * nb + n))
    return pl.pallas_call(
        functools.partial(_conv_in_kernel, s_len // tm),
        grid=(n_tok // tm, nb),
        in_specs=[
            pl.BlockSpec((tm, d), lambda i, n: (i, 0)),
            pl.BlockSpec((1, d), lambda i, n: (0, 0)),
            w_spec(0), w_spec(1), w_spec(2), w_spec(3),
            pl.BlockSpec((conv_w.shape[0], tn), lambda i, n: (0, n)),
        ],
        out_specs=pl.BlockSpec((tm, tn), lambda i, n: (i, n)),
        out_shape=jax.ShapeDtypeStruct((n_tok, width), jnp.bfloat16),
        scratch_shapes=[
            pltpu.VMEM((tm, d), jnp.bfloat16),
            pltpu.VMEM((nb, 8, tn), jnp.float32),
        ],
        compiler_params=_params(("arbitrary", "arbitrary")),
        name="conv_in_proj",
    )(h2d, g, w_in, w_in, w_in, w_in, conv_w)


def kernel(x, p, positions, norm_g, a_w_in, a_q_norm_g, a_kv_norm_g, a_w_q_up, a_w_kv_up,
           a_qn_g, a_kn_g, a_w_out, b_w_in, b_conv_w, b_w_out, ple_w_proj, ple_norm_g,
           ple_w_gate, ple_b_gate):
    bsz, s_len, d = x.shape
    depth = p.shape[0]
    ple_dim = p.shape[-1]
    assert depth == 2 and a_w_in.shape[0] == 1 and b_w_in.shape[0] == 1
    q_lora = a_q_norm_g.shape[-1]
    kv_lora = a_kv_norm_g.shape[-1]
    qk = a_qn_g.shape[-1]
    att_width = a_w_out.shape[1]
    rope = a_w_in.shape[-1] - q_lora - kv_lora - att_width
    nope = qk - rope
    n_heads = a_w_q_up.shape[-1] // qk
    v_dim = a_w_kv_up.shape[-1] // n_heads - nope
    assert n_heads * v_dim == att_width
    n_tok = bsz * s_len
    bf = jnp.bfloat16

    tm, tm_proj, bq, nsub, bk, tm_conv, tn_conv = _tile_plan(s_len, b_w_out.shape[1])
    assert s_len % tm == 0 and s_len % tm_proj == 0 and s_len % tm_conv == 0

    w_in0 = a_w_in[0]
    n_lat = q_lora + kv_lora
    w_lat = w_in0[:, :n_lat + rope].astype(bf)
    w_gate0 = w_in0[:, n_lat + rope:].astype(bf)
    wq_t = a_w_q_up[0].T.astype(bf)
    w_kv = a_w_kv_up[0].reshape(kv_lora, n_heads, nope + v_dim)
    wk = w_kv[:, :, :nope].reshape(kv_lora, n_heads * nope).astype(bf)
    wv_t = w_kv[:, :, nope:].reshape(kv_lora, n_heads * v_dim).T.astype(bf)
    gq_b = jnp.broadcast_to((a_qn_g[0] * (qk ** -0.5 * LOG2_E))[:, None], (qk, tm_proj))
    gk = a_kn_g[0]
    gk_nope = gk[None, :nope]
    gk_rope_b = jnp.broadcast_to(gk[nope:, None], (rope, tm_proj))

    h = x.reshape(n_tok, d)
    p2 = p.reshape(depth, n_tok, ple_dim)

    tables = _rope_tables(positions, rope)
    dims = (n_heads, q_lora, kv_lora, nope, rope, v_dim)
    sgate, q_t, k, v_t = _mla_proj(h, norm_g[0][None, :], w_lat, w_gate0, tables,
                                   a_q_norm_g[0][None, :], a_kv_norm_g[0][None, :],
                                   wq_t, wk, wv_t, gq_b, gk_nope, gk_rope_b,
                                   dims, bsz, s_len, tm_proj)
    act = _attention(q_t, k, v_t, sgate.reshape(bsz, s_len, att_width), bq, bk, 2, nsub)
    h = _out_ple(act.reshape(n_tok, att_width), h, p2, 0, a_w_out[0].astype(bf),
                 ple_w_proj[0].astype(bf), ple_norm_g[0][None, :], ple_w_gate[0].astype(bf),
                 ple_b_gate[0][None, :], tm)

    act = _conv_in(h, norm_g[1][None, :], b_w_in[0].astype(bf), b_conv_w[0], s_len,
                   tm_conv, tn_conv)
    h = _out_ple(act, h, p2, 1, b_w_out[0].astype(bf), ple_w_proj[1].astype(bf),
                 ple_norm_g[1][None, :], ple_w_gate[1].astype(bf), ple_b_gate[1][None, :], tm)
    return h.reshape(bsz, s_len, d)
```
